```python
import jax, jax.numpy as jnp
from jax import lax
import numpy as np

D_MODEL = 2048
BATCH = 16
SEQ = 2048
DEPTH = 2

MIX_WIDTH = D_MODEL
N_BRANCHES = 3
N_IN_SPLITS = 9
CHUNK = 128
SGU_GROUPS = 16
LRU_HEADS = 16
LRU_CONV_WIDTH = 4
LRU_C = 8.0
CONF_CONV_WIDTH = 31
N_EXPERTS = 64
N_EXPERT_GROUPS = 8
TOPK_GROUPS = 4
TOP_K = 8
EXPERT_FF = 512
SHARED_FF = 512
ROUTED_SCALE = 2.5
DISPATCH_BLOCK = 128
NORM_EPS = 1e-6

kernel_name = 'hybrid_sgu_rglru_conformer_moe_adaln'


def rms_norm(x, g):
    xf = x.astype(jnp.float32)
    y = xf * lax.rsqrt(jnp.mean(xf * xf, axis=-1, keepdims=True) + NORM_EPS) * g
    return y.astype(x.dtype)


def layer_norm(x, g, b):
    xf = x.astype(jnp.float32)
    mu = jnp.mean(xf, axis=-1, keepdims=True)
    var = jnp.mean(jnp.square(xf - mu), axis=-1, keepdims=True)
    return ((xf - mu) * lax.rsqrt(var + NORM_EPS) * g + b).astype(x.dtype)


def causal_depthwise_conv(x, w, b):
    k = w.shape[0]
    ch = x.shape[-1]
    y = lax.conv_general_dilated(
        x, w[:, None, :].astype(x.dtype), window_strides=(1,), padding=[(k - 1, 0)],
        dimension_numbers=('NWC', 'WIO', 'NWC'), feature_group_count=ch)
    return y + b


def spatial_gating(u, v, ln_g, ln_b, w_s, b_s):
    bn, s, w = v.shape
    v = layer_norm(v, ln_g, ln_b)
    vc = v.reshape(bn, s // CHUNK, CHUNK, SGU_GROUPS, w // SGU_GROUPS)
    causal = jnp.tril(jnp.ones((CHUNK, CHUNK), dtype=bool))
    ws = jnp.where(causal[None], w_s, jnp.zeros_like(w_s))
    mixed = jnp.einsum('gts,bnsgc->bntgc', ws, vc) + jnp.transpose(b_s)[:, :, None]
    return u * mixed.reshape(bn, s, w)


def rg_lru(x, w_a, b_a, w_x, b_x, lam):
    bn, s, w = x.shape
    xh = x.reshape(bn, s, LRU_HEADS, w // LRU_HEADS)
    r = jax.nn.sigmoid(jnp.einsum('bshi,hij->bshj', xh, w_a).reshape(bn, s, w) + b_a)
    i = jax.nn.sigmoid(jnp.einsum('bshi,hij->bshj', xh, w_x).reshape(bn, s, w) + b_x)
    log_a = -LRU_C * r.astype(jnp.float32) * jax.nn.softplus(-lam.astype(jnp.float32))
    a = jnp.exp(log_a)
    mult = jnp.sqrt(-jnp.expm1(2.0 * log_a))
    bterm = mult * (i * x).astype(jnp.float32)

    def combine(lhs, rhs):
        a1, b1 = lhs
        a2, b2 = rhs
        return a1 * a2, a2 * b1 + b2

    _, h = lax.associative_scan(combine, (a, bterm), axis=1)
    return h.astype(x.dtype)


def hybrid_mixer(h, w_in, b_in, sgu_ln_g, sgu_ln_b, sgu_w, sgu_b, lru_conv_w, lru_conv_b,
                 lru_wa, lru_ba, lru_wx, lru_bx, lru_lambda, conv_w, conv_b, conv_ln_g,
                 conv_ln_b, w_out, b_out):
    proj = h @ w_in + b_in
    a_u, a_v, b_x, b_g, c_a, c_b, g_a, g_b, g_c = jnp.split(proj, N_IN_SPLITS, axis=-1)
    y_a = spatial_gating(jax.nn.gelu(a_u), jax.nn.gelu(a_v), sgu_ln_g, sgu_ln_b, sgu_w, sgu_b)
    xb = causal_depthwise_conv(b_x, lru_conv_w, lru_conv_b)
    y_b = jax.nn.gelu(b_g) * rg_lru(xb, lru_wa, lru_ba, lru_wx, lru_bx, lru_lambda)
    glu = c_a * jax.nn.sigmoid(c_b)
    y_c = jax.nn.silu(layer_norm(causal_depthwise_conv(glu, conv_w, conv_b), conv_ln_g, conv_ln_b))
    y = jax.nn.sigmoid(g_a) * y_a + jax.nn.sigmoid(g_b) * y_b + jax.nn.sigmoid(g_c) * y_c
    return y @ w_out + b_out


def moe_ffn(h, router_w, router_bias, exp_gate, exp_up, exp_down, sh_gate, sh_up, sh_down):
    bn, s, d = h.shape
    t = bn * s
    hf = h.reshape(t, d)
    scores = jax.nn.sigmoid(hf.astype(jnp.float32) @ router_w.astype(jnp.float32))
    sel = scores + router_bias.astype(jnp.float32)
    grouped = sel.reshape(t, N_EXPERT_GROUPS, N_EXPERTS // N_EXPERT_GROUPS)
    group_score = jnp.sum(lax.top_k(grouped, 2)[0], axis=-1)
    _, top_groups = lax.top_k(group_score, TOPK_GROUPS)
    group_mask = jnp.sum(jax.nn.one_hot(top_groups, N_EXPERT_GROUPS, dtype=jnp.float32), axis=1) > 0
    expert_mask = jnp.repeat(group_mask, N_EXPERTS // N_EXPERT_GROUPS, axis=1)
    _, top_idx = lax.top_k(jnp.where(expert_mask, sel, -jnp.inf), TOP_K)
    top_w = jnp.take_along_axis(scores, top_idx, axis=1)
    top_w = ROUTED_SCALE * top_w / jnp.sum(top_w, axis=-1, keepdims=True)
    tk = t * TOP_K
    flat_e = top_idx.reshape(tk)
    flat_w = top_w.reshape(tk)
    flat_tok = jnp.arange(tk, dtype=jnp.int32) // TOP_K
    order = jnp.argsort(flat_e)
    sorted_e = flat_e[order]
    counts = jnp.bincount(flat_e, length=N_EXPERTS)
    starts = jnp.cumsum(counts) - counts
    padded = (counts + DISPATCH_BLOCK - 1) // DISPATCH_BLOCK * DISPATCH_BLOCK
    pends = jnp.cumsum(padded)
    pstarts = pends - padded
    dest = pstarts[sorted_e] + jnp.arange(tk, dtype=jnp.int32) - starts[sorted_e]
    n_blocks = (tk + DISPATCH_BLOCK - 1) // DISPATCH_BLOCK + N_EXPERTS
    n_rows = n_blocks * DISPATCH_BLOCK
    row_tok = jnp.zeros((n_rows,), jnp.int32).at[dest].set(flat_tok[order])
    row_w = jnp.zeros((n_rows,), h.dtype).at[dest].set(flat_w[order].astype(h.dtype))
    block_start = jnp.arange(n_blocks, dtype=jnp.int32) * DISPATCH_BLOCK
    block_e = jnp.minimum(jnp.searchsorted(pends, block_start, side='right'), N_EXPERTS - 1)

    def expert_block(args):
        tok, wt, e = args
        xb = hf[tok]
        hid = jax.nn.silu(xb @ exp_gate[e]) * (xb @ exp_up[e])
        return (hid @ exp_down[e]) * wt[:, None]

    yb = lax.map(expert_block, (row_tok.reshape(n_blocks, DISPATCH_BLOCK),
                                row_w.reshape(n_blocks, DISPATCH_BLOCK), block_e))
    routed = jax.ops.segment_sum(yb.reshape(n_rows, d), row_tok, num_segments=t)
    shared = (jax.nn.silu(hf @ sh_gate) * (hf @ sh_up)) @ sh_down
    return (routed + shared).reshape(bn, s, d)


def setup_inputs(seed: int = 0) -> dict:
    key = jax.random.key(seed)
    ks = iter(jax.random.split(key, 40))
    L, D, W, E, F, FS = DEPTH, D_MODEL, MIX_WIDTH, N_EXPERTS, EXPERT_FF, SHARED_FF
    hg = W // LRU_HEADS

    def nrm(shape, std):
        return std * jax.random.normal(next(ks), shape, jnp.float32)

    a8 = jax.random.uniform(next(ks), (L, W), jnp.float32, minval=0.9, maxval=0.999)
    a_base = a8 ** (1.0 / LRU_C)
    lru_lambda = jnp.log(a_base) - jnp.log1p(-a_base)
    return {
        'x': nrm((BATCH, SEQ, D), 1.0),
        'c': nrm((BATCH, D), 1.0),
        'ada_w': nrm((L, D, 6 * D), 0.5 * D ** -0.5),
        'ada_b': nrm((L, 6 * D), 0.02),
        'norm_mix_g': 1.0 + nrm((L, D), 0.02),
        'w_in': nrm((L, D, N_IN_SPLITS * W), D ** -0.5),
        'b_in': nrm((L, N_IN_SPLITS * W), 0.02),
        'sgu_ln_g': 1.0 + nrm((L, W), 0.02),
        'sgu_ln_b': nrm((L, W), 0.02),
        'sgu_w': nrm((L, SGU_GROUPS, CHUNK, CHUNK), 0.5 * CHUNK ** -0.5),
        'sgu_b': 1.0 + nrm((L, SGU_GROUPS, CHUNK), 0.1),
        'lru_conv_w': nrm((L, LRU_CONV_WIDTH, W), LRU_CONV_WIDTH ** -0.5),
        'lru_conv_b': nrm((L, W), 0.02),
        'lru_wa': nrm((L, LRU_HEADS, hg, hg), hg ** -0.5),
        'lru_ba': nrm((L, W), 0.02),
        'lru_wx': nrm((L, LRU_HEADS, hg, hg), hg ** -0.5),
        'lru_bx': nrm((L, W), 0.02),
        'lru_lambda': lru_lambda,
        'conv_w': nrm((L, CONF_CONV_WIDTH, W), CONF_CONV_WIDTH ** -0.5),
        'conv_b': nrm((L, W), 0.02),
        'conv_ln_g': 1.0 + nrm((L, W), 0.02),
        'conv_ln_b': nrm((L, W), 0.02),
        'w_out': nrm((L, W, D), W ** -0.5),
        'b_out': nrm((L, D), 0.02),
        'norm_ffn_g': 1.0 + nrm((L, D), 0.02),
        'router_w': nrm((L, D, E), D ** -0.5),
        'router_bias': nrm((L, E), 0.01),
        'exp_gate': nrm((L, E, D, F), D ** -0.5),
        'exp_up': nrm((L, E, D, F), D ** -0.5),
        'exp_down': nrm((L, E, F, D), F ** -0.5),
        'sh_gate': nrm((L, D, FS), D ** -0.5),
        'sh_up': nrm((L, D, FS), D ** -0.5),
        'sh_down': nrm((L, FS, D), FS ** -0.5),
        'final_norm_g': 1.0 + nrm((D,), 0.02),
    }


def reference(x, c, ada_w, ada_b, norm_mix_g, w_in, b_in, sgu_ln_g, sgu_ln_b, sgu_w, sgu_b,
              lru_conv_w, lru_conv_b, lru_wa, lru_ba, lru_wx, lru_bx, lru_lambda, conv_w, conv_b,
              conv_ln_g, conv_ln_b, w_out, b_out, norm_ffn_g, router_w, router_bias, exp_gate,
              exp_up, exp_down, sh_gate, sh_up, sh_down, final_norm_g):
    cond = jax.nn.silu(c)
    for l in range(DEPTH):
        mod = cond @ ada_w[l] + ada_b[l]
        sh_m, sc_m, g_m, sh_f, sc_f, g_f = jnp.split(mod[:, None, :], 6, axis=-1)
        h = rms_norm(x, norm_mix_g[l]) * (1.0 + sc_m) + sh_m
        x = x + g_m * hybrid_mixer(
            h, w_in[l], b_in[l], sgu_ln_g[l], sgu_ln_b[l], sgu_w[l], sgu_b[l], lru_conv_w[l],
            lru_conv_b[l], lru_wa[l], lru_ba[l], lru_wx[l], lru_bx[l], lru_lambda[l], conv_w[l],
            conv_b[l], conv_ln_g[l], conv_ln_b[l], w_out[l], b_out[l])
        h = rms_norm(x, norm_ffn_g[l]) * (1.0 + sc_f) + sh_f
        x = x + g_f * moe_ffn(h, router_w[l], router_bias[l], exp_gate[l], exp_up[l],
                              exp_down[l], sh_gate[l], sh_up[l], sh_down[l])
    return rms_norm(x, final_norm_g)
```

```python
import functools

import jax
import jax.numpy as jnp
from jax import lax
from jax.experimental import pallas as pl
from jax.experimental.pallas import tpu as pltpu

F32 = jnp.float32
BF16 = jnp.bfloat16

LANES = 128
NORM_EPS = 1e-6
CHUNK = 128
LRU_C = 8.0
LRU_CONV_WIDTH = 4
CONF_CONV_WIDTH = 31
CONV_HALO = 32
LRU_HALO = 8
N_EXPERT_GROUPS = 8
TOPK_GROUPS = 4
TOP_K = 8
ROUTED_SCALE = 2.5
EXPERT_BLOCK = 256
VMEM_LIMIT = 56 * 1024 * 1024


def _cparams(*sem):
    return pltpu.CompilerParams(dimension_semantics=sem, vmem_limit_bytes=VMEM_LIMIT)


def _gelu(x):
    return 0.5 * x * (1.0 + jnp.tanh(0.7978845608028654 * (x + 0.044715 * (x * x * x))))


def _sigmoid(x):
    return 1.0 / (1.0 + jnp.exp(-x))


def _bdot(a, b):
    return jnp.dot(a.astype(BF16), b.astype(BF16), preferred_element_type=F32)


def _ada_kernel(c_ref, w_ref, b_ref, o_ref):
    c = c_ref[...]
    cond = c * _sigmoid(c)
    o_ref[...] = _bdot(cond, w_ref[...]) + b_ref[...]


def _ada_modulation(c, ada_w, ada_b):
    n_layers, d, n_out = ada_w.shape
    bn = c.shape[0]
    tn = 1024
    return pl.pallas_call(
        _ada_kernel,
        grid=(n_layers, n_out // tn),
        in_specs=[
            pl.BlockSpec((bn, d), lambda l, j: (0, 0)),
            pl.BlockSpec((None, d, tn), lambda l, j: (l, 0, j)),
            pl.BlockSpec((None, 1, tn), lambda l, j: (l, 0, j)),
        ],
        out_specs=pl.BlockSpec((None, bn, tn), lambda l, j: (l, 0, j)),
        out_shape=jax.ShapeDtypeStruct((n_layers, bn, n_out), F32),
        compiler_params=_cparams("arbitrary", "arbitrary"),
        name="ada_modulation",
    )(c, ada_w, ada_b.reshape(n_layers, 1, n_out))


def _inproj_kernel(x_ref, mod_ref, g_ref, w_ref, b_ref, o_ref, h_ref):
    @pl.when(pl.program_id(2) == 0)
    def _():
        x = x_ref[...]
        ms = jnp.mean(x * x, axis=-1, keepdims=True)
        y = x * lax.rsqrt(ms + NORM_EPS) * g_ref[...]
        h_ref[...] = (y * (1.0 + mod_ref[1:2, :]) + mod_ref[0:1, :]).astype(BF16)

    acc = jnp.dot(h_ref[...], w_ref[...], preferred_element_type=F32)
    o_ref[...] = (acc + b_ref[...]).astype(o_ref.dtype)


def _in_projection(x, mod, g, w, b):
    bn, s, d = x.shape
    n_out = w.shape[1]
    tm = min(s, 1024)
    tn = 512
    return pl.pallas_call(
        _inproj_kernel,
        grid=(bn, s // tm, n_out // tn),
        in_specs=[
            pl.BlockSpec((None, tm, d), lambda bi, i, j: (bi, i, 0)),
            pl.BlockSpec((None, 6, d), lambda bi, i, j: (bi, 0, 0)),
            pl.BlockSpec((1, d), lambda bi, i, j: (0, 0)),
            pl.BlockSpec((d, tn), lambda bi, i, j: (0, j)),
            pl.BlockSpec((1, tn), lambda bi, i, j: (0, j)),
        ],
        out_specs=pl.BlockSpec((None, tm, tn), lambda bi, i, j: (bi, i, j)),
        out_shape=jax.ShapeDtypeStruct((bn, s, n_out), BF16),
        scratch_shapes=[pltpu.VMEM((tm, d), BF16)],
        compiler_params=_cparams("arbitrary", "arbitrary", "arbitrary"),
        name="in_projection",
    )(x, mod, g.reshape(1, d), w, b.reshape(1, n_out))


def _linear_scan(a, b, h0):
    n = a.shape[0]
    rows = lax.broadcasted_iota(jnp.int32, a.shape, 0)
    d = 1
    while d < n:
        keep = rows >= d
        b = jnp.where(keep, b + a * pltpu.roll(b, d, 0), b)
        a = jnp.where(keep, a * pltpu.roll(a, d, 0), a)
        d *= 2
    return b + a * h0


def _mixer_kernel(au_ref, av_ref, bx_ref, bg_ref, ca_ref, cb_ref, ga_ref, gb_ref, gc_ref,
                  lng_ref, lnb_ref, sw_ref, sb_ref, lcw_ref, lcb_ref, wa_ref, ba_ref, wx_ref,
                  bxb_ref, lam_ref, cw_ref, cvb_ref, cg_ref, cbt_ref,
                  y_ref, vbuf, cvbuf, gbuf, xbuf, stats, hstate):
    ts, width = vbuf.shape
    n_blk = width // LANES

    @pl.when(pl.program_id(1) == 0)
    def _():
        gbuf[0:CONV_HALO, :] = jnp.zeros((CONV_HALO, width), F32)
        xbuf[0:LRU_HALO, :] = jnp.zeros((LRU_HALO, width), F32)
        hstate[...] = jnp.zeros(hstate.shape, F32)

    stats[...] = jnp.zeros(stats.shape, F32)

    def lanes(c):
        return pl.ds(pl.multiple_of(c * LANES, LANES), LANES)

    def pass1(c, carry):
        sl = lanes(c)
        v = _gelu(av_ref[:, sl].astype(F32))
        vbuf[:, sl] = v
        stats[0] += v
        stats[1] += v * v
        glu = ca_ref[:, sl].astype(F32) * _sigmoid(cb_ref[:, sl].astype(F32))
        gbuf[CONV_HALO:CONV_HALO + ts, sl] = glu
        acc = jnp.broadcast_to(cvb_ref[:, sl], (ts, LANES))
        for k in range(CONF_CONV_WIDTH):
            off = CONV_HALO - (CONF_CONV_WIDTH - 1) + k
            acc = acc + cw_ref[k:k + 1, sl] * gbuf[off:off + ts, sl]
        cvbuf[:, sl] = acc
        stats[2] += acc
        stats[3] += acc * acc
        return carry

    lax.fori_loop(0, n_blk, pass1, 0)

    inv_w = 1.0 / width
    for q in (0, 2):
        mean = jnp.sum(stats[q], axis=-1, keepdims=True) * inv_w
        ex2 = jnp.sum(stats[q + 1], axis=-1, keepdims=True) * inv_w
        rstd = lax.rsqrt(ex2 - mean * mean + NORM_EPS)
        stats[q] = jnp.broadcast_to(mean, (ts, LANES))
        stats[q + 1] = jnp.broadcast_to(rstd, (ts, LANES))

    r_i = lax.broadcasted_iota(jnp.int32, (CHUNK, CHUNK), 0)
    c_i = lax.broadcasted_iota(jnp.int32, (CHUNK, CHUNK), 1)
    causal = r_i >= c_i

    def pass2(c, carry):
        sl = lanes(c)
        vn = (vbuf[:, sl] - stats[0]) * stats[1] * lng_ref[:, sl] + lnb_ref[:, sl]
        ws = jnp.where(causal, sw_ref[c], 0.0)
        parts = []
        for n in range(ts // CHUNK):
            parts.append(_bdot(ws, vn[n * CHUNK:(n + 1) * CHUNK, :]) + sb_ref[:, sl])
        mixed = parts[0] if len(parts) == 1 else jnp.concatenate(parts, axis=0)
        y_a = _gelu(au_ref[:, sl].astype(F32)) * mixed
        xbuf[LRU_HALO:LRU_HALO + ts, sl] = bx_ref[:, sl].astype(F32)
        xb = jnp.broadcast_to(lcb_ref[:, sl], (ts, LANES))
        for k in range(LRU_CONV_WIDTH):
            off = LRU_HALO - (LRU_CONV_WIDTH - 1) + k
            xb = xb + lcw_ref[k:k + 1, sl] * xbuf[off:off + ts, sl]
        r = _sigmoid(_bdot(xb, wa_ref[c]) + ba_ref[:, sl])
        ig = _sigmoid(_bdot(xb, wx_ref[c]) + bxb_ref[:, sl])
        lam = lam_ref[:, sl]
        softplus = jnp.maximum(-lam, 0.0) + jnp.log1p(jnp.exp(-jnp.abs(lam)))
        log_a = (-LRU_C) * r * softplus
        a = jnp.exp(log_a)
        bterm = jnp.sqrt(-jnp.tanh(log_a) * (a * a + 1.0)) * (ig * xb)
        h = _linear_scan(a, bterm, hstate[0:1, sl])
        hstate[0:1, sl] = h[ts - 1:ts, :]
        y_b = _gelu(bg_ref[:, sl].astype(F32)) * h
        z = (cvbuf[:, sl] - stats[2]) * stats[3] * cg_ref[:, sl] + cbt_ref[:, sl]
        y_c = z * _sigmoid(z)
        y = (_sigmoid(ga_ref[:, sl].astype(F32)) * y_a + _sigmoid(gb_ref[:, sl].astype(F32)) * y_b
             + _sigmoid(gc_ref[:, sl].astype(F32)) * y_c)
        y_ref[:, sl] = y.astype(y_ref.dtype)
        return carry

    lax.fori_loop(0, n_blk, pass2, 0)

    gbuf[0:CONV_HALO, :] = gbuf[ts:ts + CONV_HALO, :]
    xbuf[0:LRU_HALO, :] = xbuf[ts:ts + LRU_HALO, :]


def _hybrid_mixer(proj, sgu_ln_g, sgu_ln_b, sgu_w, sgu_b, lru_conv_w, lru_conv_b, lru_wa, lru_ba,
                  lru_wx, lru_bx, lru_lambda, conv_w, conv_b, conv_ln_g, conv_ln_b):
    bn, s, n9 = proj.shape
    width = n9 // 9
    n_grp = width // LANES
    assert sgu_w.shape == (n_grp, CHUNK, CHUNK) and lru_wa.shape == (n_grp, LANES, LANES)
    ts = CHUNK
    row = lambda v: v.reshape(1, width)
    sb = jnp.repeat(jnp.transpose(sgu_b), LANES, axis=1)
    cw = jnp.pad(conv_w, ((0, CONV_HALO - CONF_CONV_WIDTH), (0, 0)))
    lcw = jnp.pad(lru_conv_w, ((0, LRU_HALO - LRU_CONV_WIDTH), (0, 0)))

    def split(k):
        return pl.BlockSpec((None, ts, width), lambda bi, i, k=k: (bi, i, k))

    def full(shape):
        return pl.BlockSpec(shape, lambda bi, i: (0,) * len(shape))

    vec = full((1, width))
    grp = full((n_grp, LANES, LANES))
    return pl.pallas_call(
        _mixer_kernel,
        grid=(bn, s // ts),
        in_specs=[split(k) for k in range(9)] + [
            vec, vec, grp, full((CHUNK, width)), full((LRU_HALO, width)), vec, grp, vec, grp, vec,
            vec, full((CONV_HALO, width)), vec, vec, vec],
        out_specs=pl.BlockSpec((None, ts, width), lambda bi, i: (bi, i, 0)),
        out_shape=jax.ShapeDtypeStruct((bn, s, width), BF16),
        scratch_shapes=[
            pltpu.VMEM((ts, width), F32),
            pltpu.VMEM((ts, width), F32),
            pltpu.VMEM((CONV_HALO + ts, width), F32),
            pltpu.VMEM((LRU_HALO + ts, width), F32),
            pltpu.VMEM((4, ts, LANES), F32),
            pltpu.VMEM((8, width), F32),
        ],
        compiler_params=_cparams("arbitrary", "arbitrary"),
        name="hybrid_mixer",
    )(*([proj] * 9), row(sgu_ln_g), row(sgu_ln_b), sgu_w, sb, lcw, row(lru_conv_b), lru_wa,
      row(lru_ba), lru_wx, row(lru_bx), row(lru_lambda), cw, row(conv_b), row(conv_ln_g),
      row(conv_ln_b))


def _outproj_kernel(y_ref, w_ref, b_ref, x_ref, mod_ref, g_ref, xo_ref, h_ref):
    mix = jnp.dot(y_ref[...], w_ref[...], preferred_element_type=F32) + b_ref[...]
    xn = x_ref[...] + mod_ref[2:3, :] * mix
    xo_ref[...] = xn
    ms = jnp.mean(xn * xn, axis=-1, keepdims=True)
    h = xn * lax.rsqrt(ms + NORM_EPS) * g_ref[...]
    h_ref[...] = h * (1.0 + mod_ref[4:5, :]) + mod_ref[3:4, :]


def _out_projection(y, w, b, x, mod, g):
    bn, s, d = x.shape
    width = y.shape[-1]
    tm = min(s, 256)
    blk = pl.BlockSpec((None, tm, d), lambda bi, i: (bi, i, 0))
    return pl.pallas_call(
        _outproj_kernel,
        grid=(bn, s // tm),
        in_specs=[
            pl.BlockSpec((None, tm, width), lambda bi, i: (bi, i, 0)),
            pl.BlockSpec((width, d), lambda bi, i: (0, 0)),
            pl.BlockSpec((1, d), lambda bi, i: (0, 0)),
            blk,
            pl.BlockSpec((None, 6, d), lambda bi, i: (bi, 0, 0)),
            pl.BlockSpec((1, d), lambda bi, i: (0, 0)),
        ],
        out_specs=[blk, blk],
        out_shape=[jax.ShapeDtypeStruct((bn, s, d), F32), jax.ShapeDtypeStruct((bn, s, d), F32)],
        compiler_params=_cparams("arbitrary", "arbitrary"),
        name="out_projection",
    )(y, w, b.reshape(1, d), x, mod, g.reshape(1, d))


def _router_kernel(h_ref, rw_ref, rb_ref, idx_ref, w_ref, rank_ref, cnt_ref, carry_ref):
    n_exp = rw_ref.shape[0]
    tm = h_ref.shape[0]
    per_grp = n_exp // N_EXPERT_GROUPS
    neg = -jnp.inf

    @pl.when(pl.program_id(0) == 0)
    def _():
        carry_ref[...] = jnp.zeros(carry_ref.shape, F32)

    logits = lax.dot_general(rw_ref[...], h_ref[...], (((1,), (1,)), ((), ())),
                             precision=lax.Precision.HIGHEST, preferred_element_type=F32)
    scores = _sigmoid(logits)
    sel = (scores + rb_ref[...]).reshape(N_EXPERT_GROUPS, per_grp, tm)
    scores3 = scores.reshape(N_EXPERT_GROUPS, per_grp, tm)
    shape3 = (N_EXPERT_GROUPS, per_grp, tm)
    sub = lax.broadcasted_iota(jnp.int32, shape3, 1)
    gid = lax.broadcasted_iota(jnp.int32, shape3, 0)
    lin = gid * per_grp + sub

    m1 = jnp.max(sel, axis=1, keepdims=True)
    i1 = jnp.min(jnp.where(sel == m1, sub, per_grp), axis=1, keepdims=True)
    m2 = jnp.max(jnp.where(sub == i1, neg, sel), axis=1, keepdims=True)
    gscore = m1 + m2
    gidx = lax.broadcasted_iota(jnp.int32, gscore.shape, 0)
    gmask = jnp.zeros(gscore.shape, jnp.bool_)
    cur = gscore
    for _ in range(TOPK_GROUPS):
        m = jnp.max(cur, axis=0, keepdims=True)
        pick = gidx == jnp.min(jnp.where(cur == m, gidx, N_EXPERT_GROUPS), axis=0, keepdims=True)
        gmask = jnp.logical_or(gmask, pick)
        cur = jnp.where(pick, neg, cur)

    cur = jnp.where(gmask, sel, neg)
    chosen = jnp.zeros(shape3, F32)
    idx_rows, score_rows = [], []
    for _ in range(TOP_K):
        m = jnp.max(jnp.max(cur, axis=1, keepdims=True), axis=0, keepdims=True)
        first = jnp.min(jnp.min(jnp.where(cur == m, lin, n_exp), axis=1, keepdims=True),
                        axis=0, keepdims=True)
        pick = lin == first
        chosen = jnp.where(pick, 1.0, chosen)
        cur = jnp.where(pick, neg, cur)
        sc = jnp.sum(jnp.sum(jnp.where(pick, scores3, 0.0), axis=1, keepdims=True), axis=0,
                     keepdims=True)
        idx_rows.append(first.reshape(1, tm))
        score_rows.append(sc.reshape(1, tm))

    chosen2 = chosen.reshape(n_exp, tm)
    s_i = lax.broadcasted_iota(jnp.int32, (tm, tm), 0)
    t_i = lax.broadcasted_iota(jnp.int32, (tm, tm), 1)
    before = jnp.where(s_i < t_i, 1.0, 0.0).astype(BF16)
    prefix = jnp.dot(chosen2.astype(BF16), before, preferred_element_type=F32)
    rank3 = (prefix + carry_ref[...]).reshape(shape3)
    carry_ref[...] = carry_ref[...] + jnp.sum(chosen2, axis=1, keepdims=True)
    cnt_ref[...] = carry_ref[...]

    rank_rows = []
    for k in range(TOP_K):
        pick = lin == idx_rows[k].reshape(1, 1, tm)
        rk = jnp.sum(jnp.sum(jnp.where(pick, rank3, 0.0), axis=1, keepdims=True), axis=0,
                     keepdims=True)
        rank_rows.append(rk.reshape(1, tm))

    top_s = jnp.concatenate(score_rows, axis=0)
    idx_ref[...] = jnp.concatenate(idx_rows, axis=0)
    w_ref[...] = ROUTED_SCALE * top_s / jnp.sum(top_s, axis=0, keepdims=True)
    rank_ref[...] = jnp.concatenate(rank_rows, axis=0).astype(jnp.int32)


def _router(h2d, router_w, router_bias):
    t, d = h2d.shape
    n_exp = router_w.shape[1]
    tm = min(t, 256)
    row = pl.BlockSpec((TOP_K, tm), lambda i: (0, i))
    return pl.pallas_call(
        _router_kernel,
        grid=(t // tm,),
        in_specs=[
            pl.BlockSpec((tm, d), lambda i: (i, 0)),
            pl.BlockSpec((n_exp, d), lambda i: (0, 0)),
            pl.BlockSpec((n_exp, 1), lambda i: (0, 0)),
        ],
        out_specs=[row, row, row, pl.BlockSpec((n_exp, 1), lambda i: (0, 0))],
        out_shape=[
            jax.ShapeDtypeStruct((TOP_K, t), jnp.int32),
            jax.ShapeDtypeStruct((TOP_K, t), F32),
            jax.ShapeDtypeStruct((TOP_K, t), jnp.int32),
            jax.ShapeDtypeStruct((n_exp, 1), F32),
        ],
        scratch_shapes=[pltpu.VMEM((n_exp, 1), F32)],
        compiler_params=_cparams("arbitrary"),
        name="router",
    )(h2d, jnp.transpose(router_w), router_bias.reshape(n_exp, 1))


def _dispatch_kernel(dest_ref, h_ref, xs_in_ref, xs_ref, sem):
    del xs_in_ref
    tm = h_ref.shape[0]

    def row_copy(t, k):
        return pltpu.make_async_copy(h_ref.at[pl.ds(t, 1)], xs_ref.at[pl.ds(dest_ref[k, t], 1)], sem)

    def start(t, carry):
        for k in range(TOP_K):
            row_copy(t, k).start()
        return carry

    def wait(t, carry):
        for k in range(TOP_K):
            row_copy(t, k).wait()
        return carry

    lax.fori_loop(0, tm, start, 0)
    lax.fori_loop(0, tm, wait, 0)


def _dispatch(dest, h2d, n_rows):
    t, d = h2d.shape
    tm = min(t, 256)
    xs0 = jnp.zeros((n_rows, d), h2d.dtype)
    return pl.pallas_call(
        _dispatch_kernel,
        grid=(t // tm,),
        in_specs=[
            pl.BlockSpec((TOP_K, tm), lambda i: (0, i), memory_space=pltpu.SMEM),
            pl.BlockSpec((tm, d), lambda i: (i, 0)),
            pl.BlockSpec(memory_space=pl.ANY),
        ],
        out_specs=pl.BlockSpec(memory_space=pl.ANY),
        out_shape=jax.ShapeDtypeStruct((n_rows, d), h2d.dtype),
        scratch_shapes=[pltpu.SemaphoreType.DMA(())],
        input_output_aliases={2: 0},
        compiler_params=_cparams("arbitrary"),
        name="dispatch",
    )(dest, h2d, xs0)


def _expert_kernel(be_ref, nu_ref, xs_ref, wg_ref, wu_ref, wd_ref, y_ref):
    del be_ref

    @pl.when(pl.program_id(0) < nu_ref[0])
    def _():
        x = xs_ref[...].astype(BF16)
        g = jnp.dot(x, wg_ref[...], preferred_element_type=F32)
        u = jnp.dot(x, wu_ref[...], preferred_element_type=F32)
        hid = (g * _sigmoid(g)) * u
        y_ref[...] = jnp.dot(hid.astype(BF16), wd_ref[...], preferred_element_type=F32)


def _expert_ffn(block_e, n_used, xs, wg, wu, wd):
    n_rows, d = xs.shape
    n_exp, _, ff = wg.shape
    nb = n_rows // EXPERT_BLOCK

    def blk(i, be, nu):
        return (jnp.minimum(i, nu[0] - 1), 0)

    def wsel(i, be, nu):
        return (be[jnp.minimum(i, nu[0] - 1)], 0, 0)

    return pl.pallas_call(
        _expert_kernel,
        grid_spec=pltpu.PrefetchScalarGridSpec(
            num_scalar_prefetch=2,
            grid=(nb,),
            in_specs=[
                pl.BlockSpec((EXPERT_BLOCK, d), blk),
                pl.BlockSpec((None, d, ff), wsel),
                pl.BlockSpec((None, d, ff), wsel),
                pl.BlockSpec((None, ff, d), wsel),
            ],
            out_specs=pl.BlockSpec((EXPERT_BLOCK, d), blk),
        ),
        out_shape=jax.ShapeDtypeStruct((n_rows, d), F32),
        compiler_params=_cparams("arbitrary"),
        name="expert_ffn",
    )(block_e, n_used, xs, wg, wu, wd)


def _combine_kernel(dest_ref, y_hbm, h_ref, tw_ref, x_ref, mod_ref, sg_ref, su_ref, sd_ref,
                    fg_ref, o_ref, ybuf, sem, *, final_norm):
    tm = h_ref.shape[0]

    def row_copy(t, k):
        return pltpu.make_async_copy(y_hbm.at[pl.ds(dest_ref[k, t], 1)], ybuf.at[k, pl.ds(t, 1)], sem)

    def start(t, carry):
        for k in range(TOP_K):
            row_copy(t, k).start()
        return carry

    def wait(t, carry):
        for k in range(TOP_K):
            row_copy(t, k).wait()
        return carry

    lax.fori_loop(0, tm, start, 0)
    hb = h_ref[...].astype(BF16)
    g = jnp.dot(hb, sg_ref[...], preferred_element_type=F32)
    u = jnp.dot(hb, su_ref[...], preferred_element_type=F32)
    acc = jnp.dot(((g * _sigmoid(g)) * u).astype(BF16), sd_ref[...], preferred_element_type=F32)
    lax.fori_loop(0, tm, wait, 0)
    for k in range(TOP_K):
        acc = acc + tw_ref[:, k:k + 1] * ybuf[k]
    xo = x_ref[...] + mod_ref[5:6, :] * acc
    if final_norm:
        ms = jnp.mean(xo * xo, axis=-1, keepdims=True)
        xo = xo * lax.rsqrt(ms + NORM_EPS) * fg_ref[...]
    o_ref[...] = xo


def _combine(dest, y, h, tw, x, mod, sg, su, sd, fg, final_norm):
    bn, s, d = x.shape
    ff = sg.shape[1]
    tm = min(s, 128)
    per_b = s // tm
    blk = pl.BlockSpec((None, tm, d), lambda bi, i: (bi, i, 0))
    return pl.pallas_call(
        functools.partial(_combine_kernel, final_norm=final_norm),
        grid=(bn, per_b),
        in_specs=[
            pl.BlockSpec((TOP_K, tm), lambda bi, i: (0, bi * per_b + i), memory_space=pltpu.SMEM),
            pl.BlockSpec(memory_space=pl.ANY),
            blk,
            pl.BlockSpec((None, tm, TOP_K), lambda bi, i: (bi, i, 0)),
            blk,
            pl.BlockSpec((None, 6, d), lambda bi, i: (bi, 0, 0)),
            pl.BlockSpec((d, ff), lambda bi, i: (0, 0)),
            pl.BlockSpec((d, ff), lambda bi, i: (0, 0)),
            pl.BlockSpec((ff, d), lambda bi, i: (0, 0)),
            pl.BlockSpec((1, d), lambda bi, i: (0, 0)),
        ],
        out_specs=blk,
        out_shape=jax.ShapeDtypeStruct((bn, s, d), F32),
        scratch_shapes=[pltpu.VMEM((TOP_K, tm, d), F32), pltpu.SemaphoreType.DMA(())],
        compiler_params=_cparams("arbitrary", "arbitrary"),
        name="combine",
    )(dest, y, h, tw, x, mod, sg, su, sd, fg.reshape(1, d))


def _moe_ffn(h, x, mod, router_w, router_bias, exp_gate, exp_up, exp_down, sh_gate, sh_up, sh_down,
             final_g, final_norm):
    bn, s, d = h.shape
    t = bn * s
    n_exp = router_w.shape[1]
    h2d = h.reshape(t, d)
    top_idx, top_w, rank, counts = _router(h2d, router_w, router_bias)
    cnt = counts.reshape(n_exp).astype(jnp.int32)
    padded = (cnt + EXPERT_BLOCK - 1) // EXPERT_BLOCK * EXPERT_BLOCK
    pends = jnp.cumsum(padded)
    pstarts = pends - padded
    dest = pstarts[top_idx] + rank
    nb = (t * TOP_K) // EXPERT_BLOCK + n_exp
    block_start = jnp.arange(nb, dtype=jnp.int32) * EXPERT_BLOCK
    block_e = jnp.minimum(jnp.searchsorted(pends, block_start, side='right'), n_exp - 1)
    n_used = (pends[-1] // EXPERT_BLOCK).reshape(1).astype(jnp.int32)
    xs = _dispatch(dest, h2d, nb * EXPERT_BLOCK)
    y = _expert_ffn(block_e.astype(jnp.int32), n_used, xs, exp_gate.astype(BF16),
                    exp_up.astype(BF16), exp_down.astype(BF16))
    tw = jnp.transpose(top_w).reshape(bn, s, TOP_K)
    return _combine(dest, y, h, tw, x, mod, sh_gate.astype(BF16), sh_up.astype(BF16),
                    sh_down.astype(BF16), final_g, final_norm)


def kernel(x, c, ada_w, ada_b, norm_mix_g, w_in, b_in, sgu_ln_g, sgu_ln_b, sgu_w, sgu_b, lru_conv_w, lru_conv_b, lru_wa, lru_ba, lru_wx, lru_bx, lru_lambda, conv_w, conv_b, conv_ln_g, conv_ln_b, w_out, b_out, norm_ffn_g, router_w, router_bias, exp_gate, exp_up, exp_down, sh_gate, sh_up, sh_down, final_norm_g):
    depth = ada_w.shape[0]
    bn, s, d = x.shape
    mod_all = _ada_modulation(c, ada_w, ada_b).reshape(depth, bn, 6, d)
    for l in range(depth):
        mod = mod_all[l]
        proj = _in_projection(x, mod, norm_mix_g[l], w_in[l].astype(BF16), b_in[l])
        y = _hybrid_mixer(proj, sgu_ln_g[l], sgu_ln_b[l], sgu_w[l], sgu_b[l], lru_conv_w[l],
                          lru_conv_b[l], lru_wa[l], lru_ba[l], lru_wx[l], lru_bx[l], lru_lambda[l],
                          conv_w[l], conv_b[l], conv_ln_g[l], conv_ln_b[l])
        x, h = _out_projection(y, w_out[l].astype(BF16), b_out[l], x, mod, norm_ffn_g[l])
        x = _moe_ffn(h, x, mod, router_w[l], router_bias[l], exp_gate[l], exp_up[l], exp_down[l],
                     sh_gate[l], sh_up[l], sh_down[l], final_norm_g, l == depth - 1)
    return x
```

```python
import functools

import jax
import jax.numpy as jnp
from jax import lax
from jax.experimental import pallas as pl
from jax.experimental.pallas import tpu as pltpu

F32 = jnp.float32
BF16 = jnp.bfloat16

LANES = 128
SUBLANES = 8
NORM_EPS = 1e-6
CHUNK = 128
LRU_C = 8.0
LRU_CONV_WIDTH = 4
CONF_CONV_WIDTH = 31
CONV_HALO = 32
LRU_HALO = 8
N_EXPERT_GROUPS = 8
TOPK_GROUPS = 4
TOP_K = 8
ROUTED_SCALE = 2.5
EXPERT_BLOCK = 256
VMEM_LIMIT = 56 * 1024 * 1024


def _cparams(*sem):
    return pltpu.CompilerParams(dimension_semantics=sem, vmem_limit_bytes=VMEM_LIMIT)


def _gelu(x):
    return 0.5 * x * (1.0 + jnp.tanh(0.7978845608028654 * (x + 0.044715 * (x * x * x))))


def _sigmoid(x):
    return 1.0 / (1.0 + jnp.exp(-x))


def _bdot(a, b):
    return jnp.dot(a.astype(BF16), b.astype(BF16), preferred_element_type=F32)


def _ada_kernel(c_ref, w_ref, b_ref, o_ref):
    c = c_ref[...]
    cond = c * _sigmoid(c)
    o_ref[...] = _bdot(cond, w_ref[...]) + b_ref[...]


def _ada_modulation(c, ada_w, ada_b):
    n_layers, d, n_out = ada_w.shape
    bn = c.shape[0]
    tn = 1024
    return pl.pallas_call(
        _ada_kernel,
        grid=(n_layers, n_out // tn),
        in_specs=[
            pl.BlockSpec((bn, d), lambda l, j: (0, 0)),
            pl.BlockSpec((None, d, tn), lambda l, j: (l, 0, j)),
            pl.BlockSpec((None, 1, tn), lambda l, j: (l, 0, j)),
        ],
        out_specs=pl.BlockSpec((None, bn, tn), lambda l, j: (l, 0, j)),
        out_shape=jax.ShapeDtypeStruct((n_layers, bn, n_out), F32),
        compiler_params=_cparams("arbitrary", "arbitrary"),
        name="ada_modulation",
    )(c, ada_w, ada_b.reshape(n_layers, 1, n_out))


def _inproj_kernel(x_ref, mod_ref, g_ref, w_ref, b_ref, o_ref, h_ref):
    @pl.when(pl.program_id(2) == 0)
    def _():
        x = x_ref[...]
        ms = jnp.mean(x * x, axis=-1, keepdims=True)
        y = x * lax.rsqrt(ms + NORM_EPS) * g_ref[...]
        h_ref[...] = (y * (1.0 + mod_ref[1:2, :]) + mod_ref[0:1, :]).astype(BF16)

    acc = jnp.dot(h_ref[...], w_ref[...], preferred_element_type=F32)
    o_ref[...] = (acc + b_ref[...]).astype(o_ref.dtype)


def _in_projection(x, mod, g, w, b):
    bn, s, d = x.shape
    n_out = w.shape[1]
    tm = min(s, 1024)
    tn = 512
    return pl.pallas_call(
        _inproj_kernel,
        grid=(bn, s // tm, n_out // tn),
        in_specs=[
            pl.BlockSpec((None, tm, d), lambda bi, i, j: (bi, i, 0)),
            pl.BlockSpec((None, 6, d), lambda bi, i, j: (bi, 0, 0)),
            pl.BlockSpec((1, d), lambda bi, i, j: (0, 0)),
            pl.BlockSpec((d, tn), lambda bi, i, j: (0, j)),
            pl.BlockSpec((1, tn), lambda bi, i, j: (0, j)),
        ],
        out_specs=pl.BlockSpec((None, tm, tn), lambda bi, i, j: (bi, i, j)),
        out_shape=jax.ShapeDtypeStruct((bn, s, n_out), BF16),
        scratch_shapes=[pltpu.VMEM((tm, d), BF16)],
        compiler_params=_cparams("arbitrary", "arbitrary", "arbitrary"),
        name="in_projection",
    )(x, mod, g.reshape(1, d), w, b.reshape(1, n_out))


def _linear_scan(a, b, h0):
    n = a.shape[0]
    rows = lax.broadcasted_iota(jnp.int32, a.shape, 0)
    d = 1
    while d < n:
        keep = rows >= d
        b = jnp.where(keep, b + a * pltpu.roll(b, d, 0), b)
        a = jnp.where(keep, a * pltpu.roll(a, d, 0), a)
        d *= 2
    return b + a * h0


def _mixer_kernel(au_ref, av_ref, bx_ref, bg_ref, ca_ref, cb_ref, ga_ref, gb_ref, gc_ref,
                  lng_ref, lnb_ref, sw_ref, sb_ref, lcw_ref, lcb_ref, wa_ref, ba_ref, wx_ref,
                  bxb_ref, lam_ref, cw_ref, cvb_ref, cg_ref, cbt_ref,
                  y_ref, vbuf, cvbuf, gbuf, xbuf, stats, hstate, shbuf):
    ts, width = vbuf.shape
    n_blk = width // LANES

    @pl.when(pl.program_id(1) == 0)
    def _():
        gbuf[0:CONV_HALO, :] = jnp.zeros((CONV_HALO, width), F32)
        xbuf[0:LRU_HALO, :] = jnp.zeros((LRU_HALO, width), F32)
        hstate[...] = jnp.zeros(hstate.shape, F32)

    stats[...] = jnp.zeros(stats.shape, F32)

    def lanes(c):
        return pl.ds(pl.multiple_of(c * LANES, LANES), LANES)

    def pass1(c, carry):
        sl = lanes(c)
        v = _gelu(av_ref[:, sl].astype(F32))
        vbuf[:, sl] = v
        stats[0] += v
        stats[1] += v * v
        glu = ca_ref[:, sl].astype(F32) * _sigmoid(cb_ref[:, sl].astype(F32))
        gbuf[CONV_HALO:CONV_HALO + ts, sl] = glu
        for b in range(1, SUBLANES):
            shbuf[b - 1, SUBLANES:, :] = gbuf[SUBLANES - b:CONV_HALO + ts - b, sl]
        acc = jnp.broadcast_to(cvb_ref[:, sl], (ts, LANES))
        for lag in range(CONF_CONV_WIDTH):
            a, b = divmod(lag, SUBLANES)
            off = CONV_HALO - SUBLANES * a
            tap = gbuf[off:off + ts, sl] if b == 0 else shbuf[b - 1, off:off + ts, :]
            k = CONF_CONV_WIDTH - 1 - lag
            acc = acc + cw_ref[k:k + 1, sl] * tap
        cvbuf[:, sl] = acc
        stats[2] += acc
        stats[3] += acc * acc
        return carry

    lax.fori_loop(0, n_blk, pass1, 0)

    inv_w = 1.0 / width
    for q in (0, 2):
        mean = jnp.sum(stats[q], axis=-1, keepdims=True) * inv_w
        ex2 = jnp.sum(stats[q + 1], axis=-1, keepdims=True) * inv_w
        rstd = lax.rsqrt(ex2 - mean * mean + NORM_EPS)
        stats[q] = jnp.broadcast_to(mean, (ts, LANES))
        stats[q + 1] = jnp.broadcast_to(rstd, (ts, LANES))

    r_i = lax.broadcasted_iota(jnp.int32, (CHUNK, CHUNK), 0)
    c_i = lax.broadcasted_iota(jnp.int32, (CHUNK, CHUNK), 1)
    causal = r_i >= c_i

    def pass2(c, carry):
        sl = lanes(c)
        vn = (vbuf[:, sl] - stats[0]) * stats[1] * lng_ref[:, sl] + lnb_ref[:, sl]
        ws = jnp.where(causal, sw_ref[c], 0.0)
        parts = []
        for n in range(ts // CHUNK):
            parts.append(_bdot(ws, vn[n * CHUNK:(n + 1) * CHUNK, :]) + sb_ref[:, sl])
        mixed = parts[0] if len(parts) == 1 else jnp.concatenate(parts, axis=0)
        y_a = _gelu(au_ref[:, sl].astype(F32)) * mixed
        xbuf[LRU_HALO:LRU_HALO + ts, sl] = bx_ref[:, sl].astype(F32)
        xb = jnp.broadcast_to(lcb_ref[:, sl], (ts, LANES))
        for k in range(LRU_CONV_WIDTH):
            off = LRU_HALO - (LRU_CONV_WIDTH - 1) + k
            xb = xb + lcw_ref[k:k + 1, sl] * xbuf[off:off + ts, sl]
        r = _sigmoid(_bdot(xb, wa_ref[c]) + ba_ref[:, sl])
        ig = _sigmoid(_bdot(xb, wx_ref[c]) + bxb_ref[:, sl])
        lam = lam_ref[:, sl]
        softplus = jnp.maximum(-lam, 0.0) + jnp.log1p(jnp.exp(-jnp.abs(lam)))
        log_a = (-LRU_C) * r * softplus
        a = jnp.exp(log_a)
        bterm = jnp.sqrt(-jnp.tanh(log_a) * (a * a + 1.0)) * (ig * xb)
        h = _linear_scan(a, bterm, hstate[0:1, sl])
        hstate[0:1, sl] = h[ts - 1:ts, :]
        y_b = _gelu(bg_ref[:, sl].astype(F32)) * h
        z = (cvbuf[:, sl] - stats[2]) * stats[3] * cg_ref[:, sl] + cbt_ref[:, sl]
        y_c = z * _sigmoid(z)
        y = (_sigmoid(ga_ref[:, sl].astype(F32)) * y_a + _sigmoid(gb_ref[:, sl].astype(F32)) * y_b
             + _sigmoid(gc_ref[:, sl].astype(F32)) * y_c)
        y_ref[:, sl] = y.astype(y_ref.dtype)
        return carry

    lax.fori_loop(0, n_blk, pass2, 0)

    gbuf[0:CONV_HALO, :] = gbuf[ts:ts + CONV_HALO, :]
    xbuf[0:LRU_HALO, :] = xbuf[ts:ts + LRU_HALO, :]


def _hybrid_mixer(proj, sgu_ln_g, sgu_ln_b, sgu_w, sgu_b, lru_conv_w, lru_conv_b, lru_wa, lru_ba,
                  lru_wx, lru_bx, lru_lambda, conv_w, conv_b, conv_ln_g, conv_ln_b):
    bn, s, n9 = proj.shape
    width = n9 // 9
    n_grp = width // LANES
    assert sgu_w.shape == (n_grp, CHUNK, CHUNK) and lru_wa.shape == (n_grp, LANES, LANES)
    ts = CHUNK
    row = lambda v: v.reshape(1, width)
    sb = jnp.repeat(jnp.transpose(sgu_b), LANES, axis=1)
    cw = jnp.pad(conv_w, ((0, CONV_HALO - CONF_CONV_WIDTH), (0, 0)))
    lcw = jnp.pad(lru_conv_w, ((0, LRU_HALO - LRU_CONV_WIDTH), (0, 0)))

    def split(k):
        return pl.BlockSpec((None, ts, width), lambda bi, i, k=k: (bi, i, k))

    def full(shape):
        return pl.BlockSpec(shape, lambda bi, i: (0,) * len(shape))

    vec = full((1, width))
    grp = full((n_grp, LANES, LANES))
    return pl.pallas_call(
        _mixer_kernel,
        grid=(bn, s // ts),
        in_specs=[split(k) for k in range(9)] + [
            vec, vec, grp, full((CHUNK, width)), full((LRU_HALO, width)), vec, grp, vec, grp, vec,
            vec, full((CONV_HALO, width)), vec, vec, vec],
        out_specs=pl.BlockSpec((None, ts, width), lambda bi, i: (bi, i, 0)),
        out_shape=jax.ShapeDtypeStruct((bn, s, width), BF16),
        scratch_shapes=[
            pltpu.VMEM((ts, width), F32),
            pltpu.VMEM((ts, width), F32),
            pltpu.VMEM((CONV_HALO + ts, width), F32),
            pltpu.VMEM((LRU_HALO + ts, width), F32),
            pltpu.VMEM((4, ts, LANES), F32),
            pltpu.VMEM((8, width), F32),
            pltpu.VMEM((SUBLANES - 1, CONV_HALO + ts, LANES), F32),
        ],
        compiler_params=_cparams("arbitrary", "arbitrary"),
        name="hybrid_mixer",
    )(*([proj] * 9), row(sgu_ln_g), row(sgu_ln_b), sgu_w, sb, lcw, row(lru_conv_b), lru_wa,
      row(lru_ba), lru_wx, row(lru_bx), row(lru_lambda), cw, row(conv_b), row(conv_ln_g),
      row(conv_ln_b))


HI_MASK = 0xFFFF0000


def _pack_halves(v):
    n = v.shape[-1] // 2
    lo = lax.bitcast_convert_type(v[:, :n].astype(BF16).astype(F32), jnp.uint32)
    hi = lax.bitcast_convert_type(v[:, n:].astype(BF16).astype(F32), jnp.uint32)
    return (lo >> jnp.uint32(16)) | (hi & jnp.uint32(HI_MASK))


def _unpack_halves(p):
    lo = lax.bitcast_convert_type(p << jnp.uint32(16), F32)
    hi = lax.bitcast_convert_type(p & jnp.uint32(HI_MASK), F32)
    return lo, hi


def _outproj_kernel(y_ref, w_ref, b_ref, x_ref, mod_ref, g_ref, xo_ref, h_ref, hp_ref):
    mix = jnp.dot(y_ref[...], w_ref[...], preferred_element_type=F32) + b_ref[...]
    xn = x_ref[...] + mod_ref[2:3, :] * mix
    xo_ref[...] = xn
    ms = jnp.mean(xn * xn, axis=-1, keepdims=True)
    h = xn * lax.rsqrt(ms + NORM_EPS) * g_ref[...]
    h = h * (1.0 + mod_ref[4:5, :]) + mod_ref[3:4, :]
    h_ref[...] = h
    hp_ref[...] = _pack_halves(h)


def _out_projection(y, w, b, x, mod, g):
    bn, s, d = x.shape
    width = y.shape[-1]
    tm = min(s, 256)
    blk = pl.BlockSpec((None, tm, d), lambda bi, i: (bi, i, 0))
    pblk = pl.BlockSpec((None, tm, d // 2), lambda bi, i: (bi, i, 0))
    return pl.pallas_call(
        _outproj_kernel,
        grid=(bn, s // tm),
        in_specs=[
            pl.BlockSpec((None, tm, width), lambda bi, i: (bi, i, 0)),
            pl.BlockSpec((width, d), lambda bi, i: (0, 0)),
            pl.BlockSpec((1, d), lambda bi, i: (0, 0)),
            blk,
            pl.BlockSpec((None, 6, d), lambda bi, i: (bi, 0, 0)),
            pl.BlockSpec((1, d), lambda bi, i: (0, 0)),
        ],
        out_specs=[blk, blk, pblk],
        out_shape=[jax.ShapeDtypeStruct((bn, s, d), F32), jax.ShapeDtypeStruct((bn, s, d), F32),
                   jax.ShapeDtypeStruct((bn, s, d // 2), jnp.uint32)],
        compiler_params=_cparams("arbitrary", "arbitrary"),
        name="out_projection",
    )(y, w, b.reshape(1, d), x, mod, g.reshape(1, d))


def _router_kernel(h_ref, rw_ref, rb_ref, idx_ref, w_ref, rank_ref, cnt_ref, carry_ref):
    n_exp = rw_ref.shape[0]
    tm = h_ref.shape[0]
    per_grp = n_exp // N_EXPERT_GROUPS
    neg = -jnp.inf

    @pl.when(pl.program_id(0) == 0)
    def _():
        carry_ref[...] = jnp.zeros(carry_ref.shape, F32)

    logits = lax.dot_general(rw_ref[...], h_ref[...], (((1,), (1,)), ((), ())),
                             precision=lax.Precision.HIGHEST, preferred_element_type=F32)
    scores = _sigmoid(logits)
    sel = (scores + rb_ref[...]).reshape(N_EXPERT_GROUPS, per_grp, tm)
    scores3 = scores.reshape(N_EXPERT_GROUPS, per_grp, tm)
    shape3 = (N_EXPERT_GROUPS, per_grp, tm)
    sub = lax.broadcasted_iota(jnp.int32, shape3, 1)
    gid = lax.broadcasted_iota(jnp.int32, shape3, 0)
    lin = gid * per_grp + sub

    m1 = jnp.max(sel, axis=1, keepdims=True)
    i1 = jnp.min(jnp.where(sel == m1, sub, per_grp), axis=1, keepdims=True)
    m2 = jnp.max(jnp.where(sub == i1, neg, sel), axis=1, keepdims=True)
    gscore = m1 + m2
    gidx = lax.broadcasted_iota(jnp.int32, gscore.shape, 0)
    gmask = jnp.zeros(gscore.shape, jnp.bool_)
    cur = gscore
    for _ in range(TOPK_GROUPS):
        m = jnp.max(cur, axis=0, keepdims=True)
        pick = gidx == jnp.min(jnp.where(cur == m, gidx, N_EXPERT_GROUPS), axis=0, keepdims=True)
        gmask = jnp.logical_or(gmask, pick)
        cur = jnp.where(pick, neg, cur)

    cur = jnp.where(gmask, sel, neg)
    chosen = jnp.zeros(shape3, F32)
    idx_rows, score_rows = [], []
    for _ in range(TOP_K):
        m = jnp.max(jnp.max(cur, axis=1, keepdims=True), axis=0, keepdims=True)
        first = jnp.min(jnp.min(jnp.where(cur == m, lin, n_exp), axis=1, keepdims=True),
                        axis=0, keepdims=True)
        pick = lin == first
        chosen = jnp.where(pick, 1.0, chosen)
        cur = jnp.where(pick, neg, cur)
        sc = jnp.sum(jnp.sum(jnp.where(pick, scores3, 0.0), axis=1, keepdims=True), axis=0,
                     keepdims=True)
        idx_rows.append(first.reshape(1, tm))
        score_rows.append(sc.reshape(1, tm))

    chosen2 = chosen.reshape(n_exp, tm)
    s_i = lax.broadcasted_iota(jnp.int32, (tm, tm), 0)
    t_i = lax.broadcasted_iota(jnp.int32, (tm, tm), 1)
    before = jnp.where(s_i < t_i, 1.0, 0.0).astype(BF16)
    prefix = jnp.dot(chosen2.astype(BF16), before, preferred_element_type=F32)
    rank3 = (prefix + carry_ref[...]).reshape(shape3)
    carry_ref[...] = carry_ref[...] + jnp.sum(chosen2, axis=1, keepdims=True)
    cnt_ref[...] = carry_ref[...]

    rank_rows = []
    for k in range(TOP_K):
        pick = lin == idx_rows[k].reshape(1, 1, tm)
        rk = jnp.sum(jnp.sum(jnp.where(pick, rank3, 0.0), axis=1, keepdims=True), axis=0,
                     keepdims=True)
        rank_rows.append(rk.reshape(1, tm))

    top_s = jnp.concatenate(score_rows, axis=0)
    idx_ref[...] = jnp.concatenate(idx_rows, axis=0)
    w_ref[...] = ROUTED_SCALE * top_s / jnp.sum(top_s, axis=0, keepdims=True)
    rank_ref[...] = jnp.concatenate(rank_rows, axis=0).astype(jnp.int32)


def _router(h2d, router_w, router_bias):
    t, d = h2d.shape
    n_exp = router_w.shape[1]
    tm = min(t, 256)
    row = pl.BlockSpec((TOP_K, tm), lambda i: (0, i))
    return pl.pallas_call(
        _router_kernel,
        grid=(t // tm,),
        in_specs=[
            pl.BlockSpec((tm, d), lambda i: (i, 0)),
            pl.BlockSpec((n_exp, d), lambda i: (0, 0)),
            pl.BlockSpec((n_exp, 1), lambda i: (0, 0)),
        ],
        out_specs=[row, row, row, pl.BlockSpec((n_exp, 1), lambda i: (0, 0))],
        out_shape=[
            jax.ShapeDtypeStruct((TOP_K, t), jnp.int32),
            jax.ShapeDtypeStruct((TOP_K, t), F32),
            jax.ShapeDtypeStruct((TOP_K, t), jnp.int32),
            jax.ShapeDtypeStruct((n_exp, 1), F32),
        ],
        scratch_shapes=[pltpu.VMEM((n_exp, 1), F32)],
        compiler_params=_cparams("arbitrary"),
        name="router",
    )(h2d, jnp.transpose(router_w), router_bias.reshape(n_exp, 1))


def _dispatch_kernel(tail_ref, dest_ref, h_ref, xs_ref, zbuf, sem):
    tm = h_ref.shape[0]
    n_exp = tail_ref.shape[0]

    @pl.when(pl.program_id(0) == 0)
    def _():
        zbuf[...] = jnp.zeros(zbuf.shape, zbuf.dtype)

        def tail_copy(e):
            start = pl.multiple_of(tail_ref[e], EXPERT_BLOCK)
            return pltpu.make_async_copy(zbuf, xs_ref.at[pl.ds(start, EXPERT_BLOCK)], sem)

        def zstart(e, carry):
            tail_copy(e).start()
            return carry

        def zwait(e, carry):
            tail_copy(e).wait()
            return carry

        lax.fori_loop(0, n_exp, zstart, 0)
        lax.fori_loop(0, n_exp, zwait, 0)

    def row_copy(t, k):
        return pltpu.make_async_copy(h_ref.at[pl.ds(t, 1)], xs_ref.at[pl.ds(dest_ref[k, t], 1)], sem)

    def start(t, carry):
        for k in range(TOP_K):
            row_copy(t, k).start()
        return carry

    def wait(t, carry):
        for k in range(TOP_K):
            row_copy(t, k).wait()
        return carry

    lax.fori_loop(0, tm, start, 0)
    lax.fori_loop(0, tm, wait, 0)


def _dispatch(tail_start, dest, hp2d, n_rows):
    t, half = hp2d.shape
    tm = min(t, 256)
    return pl.pallas_call(
        _dispatch_kernel,
        grid_spec=pltpu.PrefetchScalarGridSpec(
            num_scalar_prefetch=1,
            grid=(t // tm,),
            in_specs=[
                pl.BlockSpec((TOP_K, tm), lambda i, tail: (0, i), memory_space=pltpu.SMEM),
                pl.BlockSpec((tm, half), lambda i, tail: (i, 0)),
            ],
            out_specs=pl.BlockSpec(memory_space=pl.ANY),
            scratch_shapes=[pltpu.VMEM((EXPERT_BLOCK, half), hp2d.dtype), pltpu.SemaphoreType.DMA(())],
        ),
        out_shape=jax.ShapeDtypeStruct((n_rows, half), hp2d.dtype),
        compiler_params=_cparams("arbitrary"),
        name="dispatch",
    )(tail_start, dest, hp2d)


def _expert_kernel(be_ref, nu_ref, xs_ref, wg_ref, wu_ref, wd_ref, y_ref):
    del be_ref

    @pl.when(pl.program_id(0) < nu_ref[0])
    def _():
        half = xs_ref.shape[1]
        x_lo, x_hi = _unpack_halves(xs_ref[...])
        x_lo = x_lo.astype(BF16)
        x_hi = x_hi.astype(BF16)
        g = (jnp.dot(x_lo, wg_ref[0:half, :], preferred_element_type=F32)
             + jnp.dot(x_hi, wg_ref[half:, :], preferred_element_type=F32))
        u = (jnp.dot(x_lo, wu_ref[0:half, :], preferred_element_type=F32)
             + jnp.dot(x_hi, wu_ref[half:, :], preferred_element_type=F32))
        hid = ((g * _sigmoid(g)) * u).astype(BF16)
        y_ref[...] = _pack_halves(jnp.dot(hid, wd_ref[...], preferred_element_type=F32))


def _expert_ffn(block_e, n_used, xs, wg, wu, wd):
    n_rows, half = xs.shape
    n_exp, d, ff = wg.shape
    nb = n_rows // EXPERT_BLOCK

    def blk(i, be, nu):
        return (jnp.minimum(i, nu[0] - 1), 0)

    def wsel(i, be, nu):
        return (be[jnp.minimum(i, nu[0] - 1)], 0, 0)

    return pl.pallas_call(
        _expert_kernel,
        grid_spec=pltpu.PrefetchScalarGridSpec(
            num_scalar_prefetch=2,
            grid=(nb,),
            in_specs=[
                pl.BlockSpec((EXPERT_BLOCK, half), blk),
                pl.BlockSpec((None, d, ff), wsel),
                pl.BlockSpec((None, d, ff), wsel),
                pl.BlockSpec((None, ff, d), wsel),
            ],
            out_specs=pl.BlockSpec((EXPERT_BLOCK, half), blk),
        ),
        out_shape=jax.ShapeDtypeStruct((n_rows, half), jnp.uint32),
        compiler_params=_cparams("arbitrary"),
        name="expert_ffn",
    )(block_e, n_used, xs, wg, wu, wd)


def _combine_kernel(dest_ref, y_hbm, h_ref, tw_ref, x_ref, mod_ref, sg_ref, su_ref, sd_ref,
                    fg_ref, o_ref, ybuf, sem, *, final_norm):
    tm = h_ref.shape[0]

    def row_copy(t, k):
        return pltpu.make_async_copy(y_hbm.at[pl.ds(dest_ref[k, t], 1)], ybuf.at[k, pl.ds(t, 1)], sem)

    def start(t, carry):
        for k in range(TOP_K):
            row_copy(t, k).start()
        return carry

    def wait(t, carry):
        for k in range(TOP_K):
            row_copy(t, k).wait()
        return carry

    lax.fori_loop(0, tm, start, 0)
    hb = h_ref[...].astype(BF16)
    g = jnp.dot(hb, sg_ref[...], preferred_element_type=F32)
    u = jnp.dot(hb, su_ref[...], preferred_element_type=F32)
    acc = jnp.dot(((g * _sigmoid(g)) * u).astype(BF16), sd_ref[...], preferred_element_type=F32)
    lax.fori_loop(0, tm, wait, 0)
    half = ybuf.shape[-1]
    acc_lo = acc[:, :half]
    acc_hi = acc[:, half:]
    for k in range(TOP_K):
        y_lo, y_hi = _unpack_halves(ybuf[k])
        wk = tw_ref[:, k:k + 1]
        acc_lo = acc_lo + wk * y_lo
        acc_hi = acc_hi + wk * y_hi
    xo = x_ref[...] + mod_ref[5:6, :] * jnp.concatenate([acc_lo, acc_hi], axis=1)
    if final_norm:
        ms = jnp.mean(xo * xo, axis=-1, keepdims=True)
        xo = xo * lax.rsqrt(ms + NORM_EPS) * fg_ref[...]
    o_ref[...] = xo


def _combine(dest, y, h, tw, x, mod, sg, su, sd, fg, final_norm):
    bn, s, d = x.shape
    ff = sg.shape[1]
    tm = min(s, 128)
    per_b = s // tm
    blk = pl.BlockSpec((None, tm, d), lambda bi, i: (bi, i, 0))
    return pl.pallas_call(
        functools.partial(_combine_kernel, final_norm=final_norm),
        grid=(bn, per_b),
        in_specs=[
            pl.BlockSpec((TOP_K, tm), lambda bi, i: (0, bi * per_b + i), memory_space=pltpu.SMEM),
            pl.BlockSpec(memory_space=pl.ANY),
            blk,
            pl.BlockSpec((None, tm, TOP_K), lambda bi, i: (bi, i, 0)),
            blk,
            pl.BlockSpec((None, 6, d), lambda bi, i: (bi, 0, 0)),
            pl.BlockSpec((d, ff), lambda bi, i: (0, 0)),
            pl.BlockSpec((d, ff), lambda bi, i: (0, 0)),
            pl.BlockSpec((ff, d), lambda bi, i: (0, 0)),
            pl.BlockSpec((1, d), lambda bi, i: (0, 0)),
        ],
        out_specs=blk,
        out_shape=jax.ShapeDtypeStruct((bn, s, d), F32),
        scratch_shapes=[pltpu.VMEM((TOP_K, tm, d // 2), jnp.uint32), pltpu.SemaphoreType.DMA(())],
        compiler_params=_cparams("arbitrary", "arbitrary"),
        name="combine",
    )(dest, y, h, tw, x, mod, sg, su, sd, fg.reshape(1, d))


def _moe_ffn(h, hp, x, mod, router_w, router_bias, exp_gate, exp_up, exp_down, sh_gate, sh_up, sh_down,
             final_g, final_norm):
    bn, s, d = h.shape
    t = bn * s
    n_exp = router_w.shape[1]
    h2d = h.reshape(t, d)
    top_idx, top_w, rank, counts = _router(h2d, router_w, router_bias)
    cnt = counts.reshape(n_exp).astype(jnp.int32)
    padded = (cnt + EXPERT_BLOCK - 1) // EXPERT_BLOCK * EXPERT_BLOCK
    pends = jnp.cumsum(padded)
    pstarts = pends - padded
    experts = jnp.arange(n_exp, dtype=jnp.int32).reshape(n_exp, 1, 1)
    dest = jnp.sum(jnp.where(top_idx[None] == experts, pstarts.reshape(n_exp, 1, 1), 0), axis=0) + rank
    nb = (t * TOP_K) // EXPERT_BLOCK + n_exp
    block_start = jnp.arange(nb, dtype=jnp.int32) * EXPERT_BLOCK
    block_e = jnp.minimum(jnp.sum(pends[None, :] <= block_start[:, None], axis=1), n_exp - 1)
    n_used = (pends[-1] // EXPERT_BLOCK).reshape(1).astype(jnp.int32)
    tail_start = jnp.maximum(pends - EXPERT_BLOCK, 0).astype(jnp.int32)
    xs = _dispatch(tail_start, dest, hp.reshape(t, d // 2), nb * EXPERT_BLOCK)
    y = _expert_ffn(block_e.astype(jnp.int32), n_used, xs, exp_gate.astype(BF16),
                    exp_up.astype(BF16), exp_down.astype(BF16))
    tw = jnp.transpose(top_w).reshape(bn, s, TOP_K)
    return _combine(dest, y, h, tw, x, mod, sh_gate.astype(BF16), sh_up.astype(BF16),
                    sh_down.astype(BF16), final_g, final_norm)


def kernel(x, c, ada_w, ada_b, norm_mix_g, w_in, b_in, sgu_ln_g, sgu_ln_b, sgu_w, sgu_b, lru_conv_w, lru_conv_b, lru_wa, lru_ba, lru_wx, lru_bx, lru_lambda, conv_w, conv_b, conv_ln_g, conv_ln_b, w_out, b_out, norm_ffn_g, router_w, router_bias, exp_gate, exp_up, exp_down, sh_gate, sh_up, sh_down, final_norm_g):
    depth = ada_w.shape[0]
    bn, s, d = x.shape
    mod_all = _ada_modulation(c, ada_w, ada_b).reshape(depth, bn, 6, d)
    for l in range(depth):
        mod = mod_all[l]
        proj = _in_projection(x, mod, norm_mix_g[l], w_in[l].astype(BF16), b_in[l])
        y = _hybrid_mixer(proj, sgu_ln_g[l], sgu_ln_b[l], sgu_w[l], sgu_b[l], lru_conv_w[l],
                          lru_conv_b[l], lru_wa[l], lru_ba[l], lru_wx[l], lru_bx[l], lru_lambda[l],
                          conv_w[l], conv_b[l], conv_ln_g[l], conv_ln_b[l])
        x, h, hp = _out_projection(y, w_out[l].astype(BF16), b_out[l], x, mod, norm_ffn_g[l])
        x = _moe_ffn(h, hp, x, mod, router_w[l], router_bias[l], exp_gate[l], exp_up[l], exp_down[l],
                     sh_gate[l], sh_up[l], sh_down[l], final_norm_g, l == depth - 1)
    return x
```

```python
import functools

import jax
import jax.numpy as jnp
from jax import lax
from jax.experimental import pallas as pl
from jax.experimental.pallas import tpu as pltpu

F32 = jnp.float32
BF16 = jnp.bfloat16

LANES = 128
SUBLANES = 8
NORM_EPS = 1e-6
CHUNK = 128
LRU_C = 8.0
LRU_CONV_WIDTH = 4
CONF_CONV_WIDTH = 31
CONV_HALO = 32
LRU_HALO = 8
N_EXPERT_GROUPS = 8
TOPK_GROUPS = 4
TOP_K = 8
ROUTED_SCALE = 2.5
EXPERT_BLOCK = 256
VMEM_LIMIT = 56 * 1024 * 1024


def _cparams(*sem):
    return pltpu.CompilerParams(dimension_semantics=sem, vmem_limit_bytes=VMEM_LIMIT)


def _gelu(x):
    return 0.5 * x * (1.0 + jnp.tanh(0.7978845608028654 * (x + 0.044715 * (x * x * x))))


def _sigmoid(x):
    return 1.0 / (1.0 + jnp.exp(-x))


def _bdot(a, b):
    return jnp.dot(a.astype(BF16), b.astype(BF16), preferred_element_type=F32)


def _ada_kernel(c_ref, w_ref, b_ref, o_ref):
    c = c_ref[...]
    cond = c * _sigmoid(c)
    o_ref[...] = _bdot(cond, w_ref[...]) + b_ref[...]


def _ada_modulation(c, ada_w, ada_b):
    n_layers, d, n_out = ada_w.shape
    bn = c.shape[0]
    tn = 1024
    return pl.pallas_call(
        _ada_kernel,
        grid=(n_layers, n_out // tn),
        in_specs=[
            pl.BlockSpec((bn, d), lambda l, j: (0, 0)),
            pl.BlockSpec((None, d, tn), lambda l, j: (l, 0, j)),
            pl.BlockSpec((None, 1, tn), lambda l, j: (l, 0, j)),
        ],
        out_specs=pl.BlockSpec((None, bn, tn), lambda l, j: (l, 0, j)),
        out_shape=jax.ShapeDtypeStruct((n_layers, bn, n_out), F32),
        compiler_params=_cparams("arbitrary", "arbitrary"),
        name="ada_modulation",
    )(c, ada_w, ada_b.reshape(n_layers, 1, n_out))


def _inproj_kernel(x_ref, mod_ref, g_ref, w_ref, b_ref, o_ref, h_ref):
    @pl.when(pl.program_id(2) == 0)
    def _():
        x = x_ref[...]
        ms = jnp.mean(x * x, axis=-1, keepdims=True)
        y = x * lax.rsqrt(ms + NORM_EPS) * g_ref[...]
        h_ref[...] = (y * (1.0 + mod_ref[1:2, :]) + mod_ref[0:1, :]).astype(BF16)

    acc = jnp.dot(h_ref[...], w_ref[...], preferred_element_type=F32)
    o_ref[...] = (acc + b_ref[...]).astype(o_ref.dtype)


def _in_projection(x, mod, g, w_all, layer, b):
    bn, s, d = x.shape
    n_out = w_all.shape[2]
    tm = min(s, 1024)
    tn = 1024
    return pl.pallas_call(
        _inproj_kernel,
        grid=(bn, s // tm, n_out // tn),
        in_specs=[
            pl.BlockSpec((None, tm, d), lambda bi, i, j: (bi, i, 0)),
            pl.BlockSpec((None, 6, d), lambda bi, i, j: (bi, 0, 0)),
            pl.BlockSpec((1, d), lambda bi, i, j: (0, 0)),
            pl.BlockSpec((None, d, tn), lambda bi, i, j: (layer, 0, j)),
            pl.BlockSpec((1, tn), lambda bi, i, j: (0, j)),
        ],
        out_specs=pl.BlockSpec((None, tm, tn), lambda bi, i, j: (bi, i, j)),
        out_shape=jax.ShapeDtypeStruct((bn, s, n_out), BF16),
        scratch_shapes=[pltpu.VMEM((tm, d), BF16)],
        compiler_params=_cparams("arbitrary", "arbitrary", "arbitrary"),
        name="in_projection",
    )(x, mod, g.reshape(1, d), w_all, b.reshape(1, n_out))


def _linear_scan(a, b, h0):
    n = a.shape[0]
    rows = lax.broadcasted_iota(jnp.int32, a.shape, 0)
    d = 1
    while d < n:
        keep = rows >= d
        b = jnp.where(keep, b + a * pltpu.roll(b, d, 0), b)
        a = jnp.where(keep, a * pltpu.roll(a, d, 0), a)
        d *= 2
    return b + a * h0


def _mixer_kernel(au_ref, av_ref, bx_ref, bg_ref, ca_ref, cb_ref, ga_ref, gb_ref, gc_ref,
                  lng_ref, lnb_ref, sw_ref, sb_ref, lcw_ref, lcb_ref, wa_ref, ba_ref, wx_ref,
                  bxb_ref, lam_ref, cw_ref, cvb_ref, cg_ref, cbt_ref,
                  y_ref, vbuf, cvbuf, gbuf, xbuf, stats, hstate, shbuf):
    ts, width = vbuf.shape
    n_blk = width // LANES

    @pl.when(pl.program_id(1) == 0)
    def _():
        gbuf[0:CONV_HALO, :] = jnp.zeros((CONV_HALO, width), F32)
        xbuf[0:LRU_HALO, :] = jnp.zeros((LRU_HALO, width), F32)
        hstate[...] = jnp.zeros(hstate.shape, F32)

    stats[...] = jnp.zeros(stats.shape, F32)

    def lanes(c):
        return pl.ds(pl.multiple_of(c * LANES, LANES), LANES)

    def pass1(c, carry):
        sl = lanes(c)
        v = _gelu(av_ref[:, sl].astype(F32))
        vbuf[:, sl] = v
        stats[0] += v
        stats[1] += v * v
        glu = ca_ref[:, sl].astype(F32) * _sigmoid(cb_ref[:, sl].astype(F32))
        gbuf[CONV_HALO:CONV_HALO + ts, sl] = glu
        for b in range(1, SUBLANES):
            shbuf[b - 1, SUBLANES:, :] = gbuf[SUBLANES - b:CONV_HALO + ts - b, sl]
        acc = jnp.broadcast_to(cvb_ref[:, sl], (ts, LANES))
        for lag in range(CONF_CONV_WIDTH):
            a, b = divmod(lag, SUBLANES)
            off = CONV_HALO - SUBLANES * a
            tap = gbuf[off:off + ts, sl] if b == 0 else shbuf[b - 1, off:off + ts, :]
            k = CONF_CONV_WIDTH - 1 - lag
            acc = acc + cw_ref[k:k + 1, sl] * tap
        cvbuf[:, sl] = acc
        stats[2] += acc
        stats[3] += acc * acc
        return carry

    lax.fori_loop(0, n_blk, pass1, 0)

    inv_w = 1.0 / width
    for q in (0, 2):
        mean = jnp.sum(stats[q], axis=-1, keepdims=True) * inv_w
        ex2 = jnp.sum(stats[q + 1], axis=-1, keepdims=True) * inv_w
        rstd = lax.rsqrt(ex2 - mean * mean + NORM_EPS)
        stats[q] = jnp.broadcast_to(mean, (ts, LANES))
        stats[q + 1] = jnp.broadcast_to(rstd, (ts, LANES))

    r_i = lax.broadcasted_iota(jnp.int32, (CHUNK, CHUNK), 0)
    c_i = lax.broadcasted_iota(jnp.int32, (CHUNK, CHUNK), 1)
    causal = r_i >= c_i

    def pass2(c, carry):
        sl = lanes(c)
        vn = (vbuf[:, sl] - stats[0]) * stats[1] * lng_ref[:, sl] + lnb_ref[:, sl]
        ws = jnp.where(causal, sw_ref[c], 0.0)
        parts = []
        for n in range(ts // CHUNK):
            parts.append(_bdot(ws, vn[n * CHUNK:(n + 1) * CHUNK, :]) + sb_ref[:, sl])
        mixed = parts[0] if len(parts) == 1 else jnp.concatenate(parts, axis=0)
        y_a = _gelu(au_ref[:, sl].astype(F32)) * mixed
        xbuf[LRU_HALO:LRU_HALO + ts, sl] = bx_ref[:, sl].astype(F32)
        xb = jnp.broadcast_to(lcb_ref[:, sl], (ts, LANES))
        for k in range(LRU_CONV_WIDTH):
            off = LRU_HALO - (LRU_CONV_WIDTH - 1) + k
            xb = xb + lcw_ref[k:k + 1, sl] * xbuf[off:off + ts, sl]
        r = _sigmoid(_bdot(xb, wa_ref[c]) + ba_ref[:, sl])
        ig = _sigmoid(_bdot(xb, wx_ref[c]) + bxb_ref[:, sl])
        lam = lam_ref[:, sl]
        softplus = jnp.maximum(-lam, 0.0) + jnp.log1p(jnp.exp(-jnp.abs(lam)))
        log_a = (-LRU_C) * r * softplus
        a = jnp.exp(log_a)
        bterm = jnp.sqrt(-jnp.tanh(log_a) * (a * a + 1.0)) * (ig * xb)
        h = _linear_scan(a, bterm, hstate[0:1, sl])
        hstate[0:1, sl] = h[ts - 1:ts, :]
        y_b = _gelu(bg_ref[:, sl].astype(F32)) * h
        z = (cvbuf[:, sl] - stats[2]) * stats[3] * cg_ref[:, sl] + cbt_ref[:, sl]
        y_c = z * _sigmoid(z)
        y = (_sigmoid(ga_ref[:, sl].astype(F32)) * y_a + _sigmoid(gb_ref[:, sl].astype(F32)) * y_b
             + _sigmoid(gc_ref[:, sl].astype(F32)) * y_c)
        y_ref[:, sl] = y.astype(y_ref.dtype)
        return carry

    lax.fori_loop(0, n_blk, pass2, 0)

    gbuf[0:CONV_HALO, :] = gbuf[ts:ts + CONV_HALO, :]
    xbuf[0:LRU_HALO, :] = xbuf[ts:ts + LRU_HALO, :]


def _hybrid_mixer(proj, sgu_ln_g, sgu_ln_b, sgu_w, sgu_b, lru_conv_w, lru_conv_b, lru_wa, lru_ba,
                  lru_wx, lru_bx, lru_lambda, conv_w, conv_b, conv_ln_g, conv_ln_b):
    bn, s, n9 = proj.shape
    width = n9 // 9
    n_grp = width // LANES
    assert sgu_w.shape == (n_grp, CHUNK, CHUNK) and lru_wa.shape == (n_grp, LANES, LANES)
    ts = CHUNK
    row = lambda v: v.reshape(1, width)
    sb = jnp.repeat(jnp.transpose(sgu_b), LANES, axis=1)
    cw = jnp.pad(conv_w, ((0, CONV_HALO - CONF_CONV_WIDTH), (0, 0)))
    lcw = jnp.pad(lru_conv_w, ((0, LRU_HALO - LRU_CONV_WIDTH), (0, 0)))

    def split(k):
        return pl.BlockSpec((None, ts, width), lambda bi, i, k=k: (bi, i, k))

    def full(shape):
        return pl.BlockSpec(shape, lambda bi, i: (0,) * len(shape))

    vec = full((1, width))
    grp = full((n_grp, LANES, LANES))
    return pl.pallas_call(
        _mixer_kernel,
        grid=(bn, s // ts),
        in_specs=[split(k) for k in range(9)] + [
            vec, vec, grp, full((CHUNK, width)), full((LRU_HALO, width)), vec, grp, vec, grp, vec,
            vec, full((CONV_HALO, width)), vec, vec, vec],
        out_specs=pl.BlockSpec((None, ts, width), lambda bi, i: (bi, i, 0)),
        out_shape=jax.ShapeDtypeStruct((bn, s, width), BF16),
        scratch_shapes=[
            pltpu.VMEM((ts, width), F32),
            pltpu.VMEM((ts, width), F32),
            pltpu.VMEM((CONV_HALO + ts, width), F32),
            pltpu.VMEM((LRU_HALO + ts, width), F32),
            pltpu.VMEM((4, ts, LANES), F32),
            pltpu.VMEM((8, width), F32),
            pltpu.VMEM((SUBLANES - 1, CONV_HALO + ts, LANES), F32),
        ],
        compiler_params=_cparams("arbitrary", "arbitrary"),
        name="hybrid_mixer",
    )(*([proj] * 9), row(sgu_ln_g), row(sgu_ln_b), sgu_w, sb, lcw, row(lru_conv_b), lru_wa,
      row(lru_ba), lru_wx, row(lru_bx), row(lru_lambda), cw, row(conv_b), row(conv_ln_g),
      row(conv_ln_b))


HI_MASK = 0xFFFF0000


def _pack_halves(v):
    n = v.shape[-1] // 2
    lo = lax.bitcast_convert_type(v[:, :n].astype(BF16).astype(F32), jnp.uint32)
    hi = lax.bitcast_convert_type(v[:, n:].astype(BF16).astype(F32), jnp.uint32)
    return (lo >> jnp.uint32(16)) | (hi & jnp.uint32(HI_MASK))


def _unpack_halves(p):
    lo = lax.bitcast_convert_type(p << jnp.uint32(16), F32)
    hi = lax.bitcast_convert_type(p & jnp.uint32(HI_MASK), F32)
    return lo, hi


def _outproj_kernel(y_ref, w_ref, b_ref, x_ref, mod_ref, g_ref, xo_ref, h_ref, hp_ref):
    mix = jnp.dot(y_ref[...], w_ref[...], preferred_element_type=F32) + b_ref[...]
    xn = x_ref[...] + mod_ref[2:3, :] * mix
    xo_ref[...] = xn
    ms = jnp.mean(xn * xn, axis=-1, keepdims=True)
    h = xn * lax.rsqrt(ms + NORM_EPS) * g_ref[...]
    h = h * (1.0 + mod_ref[4:5, :]) + mod_ref[3:4, :]
    h_ref[...] = h
    hp_ref[...] = _pack_halves(h)


def _out_projection(y, w_all, layer, b, x, mod, g):
    bn, s, d = x.shape
    width = y.shape[-1]
    tm = min(s, 256)
    blk = pl.BlockSpec((None, tm, d), lambda bi, i: (bi, i, 0))
    pblk = pl.BlockSpec((None, tm, d // 2), lambda bi, i: (bi, i, 0))
    return pl.pallas_call(
        _outproj_kernel,
        grid=(bn, s // tm),
        in_specs=[
            pl.BlockSpec((None, tm, width), lambda bi, i: (bi, i, 0)),
            pl.BlockSpec((None, width, d), lambda bi, i: (layer, 0, 0)),
            pl.BlockSpec((1, d), lambda bi, i: (0, 0)),
            blk,
            pl.BlockSpec((None, 6, d), lambda bi, i: (bi, 0, 0)),
            pl.BlockSpec((1, d), lambda bi, i: (0, 0)),
        ],
        out_specs=[blk, blk, pblk],
        out_shape=[jax.ShapeDtypeStruct((bn, s, d), F32), jax.ShapeDtypeStruct((bn, s, d), F32),
                   jax.ShapeDtypeStruct((bn, s, d // 2), jnp.uint32)],
        compiler_params=_cparams("arbitrary", "arbitrary"),
        name="out_projection",
    )(y, w_all, b.reshape(1, d), x, mod, g.reshape(1, d))


def _router_kernel(h_ref, rw_ref, rb_ref, idx_ref, w_ref, rank_ref, cnt_ref, carry_ref):
    n_exp = rw_ref.shape[0]
    tm = h_ref.shape[0]
    per_grp = n_exp // N_EXPERT_GROUPS
    neg = -jnp.inf

    @pl.when(pl.program_id(0) == 0)
    def _():
        carry_ref[...] = jnp.zeros(carry_ref.shape, F32)

    logits = lax.dot_general(rw_ref[...], h_ref[...], (((1,), (1,)), ((), ())),
                             precision=lax.Precision.HIGHEST, preferred_element_type=F32)
    scores = _sigmoid(logits)
    sel = (scores + rb_ref[...]).reshape(N_EXPERT_GROUPS, per_grp, tm)
    scores3 = scores.reshape(N_EXPERT_GROUPS, per_grp, tm)
    shape3 = (N_EXPERT_GROUPS, per_grp, tm)
    sub = lax.broadcasted_iota(jnp.int32, shape3, 1)
    gid = lax.broadcasted_iota(jnp.int32, shape3, 0)
    lin = gid * per_grp + sub

    m1 = jnp.max(sel, axis=1, keepdims=True)
    i1 = jnp.min(jnp.where(sel == m1, sub, per_grp), axis=1, keepdims=True)
    m2 = jnp.max(jnp.where(sub == i1, neg, sel), axis=1, keepdims=True)
    gscore = m1 + m2
    gidx = lax.broadcasted_iota(jnp.int32, gscore.shape, 0)
    gmask = jnp.zeros(gscore.shape, jnp.bool_)
    cur = gscore
    for _ in range(TOPK_GROUPS):
        m = jnp.max(cur, axis=0, keepdims=True)
        pick = gidx == jnp.min(jnp.where(cur == m, gidx, N_EXPERT_GROUPS), axis=0, keepdims=True)
        gmask = jnp.logical_or(gmask, pick)
        cur = jnp.where(pick, neg, cur)

    cur = jnp.where(gmask, sel, neg)
    chosen = jnp.zeros(shape3, F32)
    idx_rows, score_rows = [], []
    for _ in range(TOP_K):
        m = jnp.max(jnp.max(cur, axis=1, keepdims=True), axis=0, keepdims=True)
        first = jnp.min(jnp.min(jnp.where(cur == m, lin, n_exp), axis=1, keepdims=True),
                        axis=0, keepdims=True)
        pick = lin == first
        chosen = jnp.where(pick, 1.0, chosen)
        cur = jnp.where(pick, neg, cur)
        sc = jnp.sum(jnp.sum(jnp.where(pick, scores3, 0.0), axis=1, keepdims=True), axis=0,
                     keepdims=True)
        idx_rows.append(first.reshape(1, tm))
        score_rows.append(sc.reshape(1, tm))

    chosen2 = chosen.reshape(n_exp, tm)
    s_i = lax.broadcasted_iota(jnp.int32, (tm, tm), 0)
    t_i = lax.broadcasted_iota(jnp.int32, (tm, tm), 1)
    before = jnp.where(s_i < t_i, 1.0, 0.0).astype(BF16)
    prefix = jnp.dot(chosen2.astype(BF16), before, preferred_element_type=F32)
    rank3 = (prefix + carry_ref[...]).reshape(shape3)
    carry_ref[...] = carry_ref[...] + jnp.sum(chosen2, axis=1, keepdims=True)
    cnt_ref[...] = carry_ref[...]

    rank_rows = []
    for k in range(TOP_K):
        pick = lin == idx_rows[k].reshape(1, 1, tm)
        rk = jnp.sum(jnp.sum(jnp.where(pick, rank3, 0.0), axis=1, keepdims=True), axis=0,
                     keepdims=True)
        rank_rows.append(rk.reshape(1, tm))

    top_s = jnp.concatenate(score_rows, axis=0)
    idx_ref[...] = jnp.concatenate(idx_rows, axis=0)
    w_ref[...] = ROUTED_SCALE * top_s / jnp.sum(top_s, axis=0, keepdims=True)
    rank_ref[...] = jnp.concatenate(rank_rows, axis=0).astype(jnp.int32)


def _router(h2d, router_w, router_bias):
    t, d = h2d.shape
    n_exp = router_w.shape[1]
    tm = min(t, 256)
    row = pl.BlockSpec((TOP_K, tm), lambda i: (0, i))
    return pl.pallas_call(
        _router_kernel,
        grid=(t // tm,),
        in_specs=[
            pl.BlockSpec((tm, d), lambda i: (i, 0)),
            pl.BlockSpec((n_exp, d), lambda i: (0, 0)),
            pl.BlockSpec((n_exp, 1), lambda i: (0, 0)),
        ],
        out_specs=[row, row, row, pl.BlockSpec((n_exp, 1), lambda i: (0, 0))],
        out_shape=[
            jax.ShapeDtypeStruct((TOP_K, t), jnp.int32),
            jax.ShapeDtypeStruct((TOP_K, t), F32),
            jax.ShapeDtypeStruct((TOP_K, t), jnp.int32),
            jax.ShapeDtypeStruct((n_exp, 1), F32),
        ],
        scratch_shapes=[pltpu.VMEM((n_exp, 1), F32)],
        compiler_params=_cparams("arbitrary"),
        name="router",
    )(h2d, jnp.transpose(router_w), router_bias.reshape(n_exp, 1))


def _dispatch_kernel(tail_ref, dest_ref, h_ref, xs_ref, zbuf, sem):
    tm, row_words = h_ref.shape
    n_exp = tail_ref.shape[0]

    @pl.when(pl.program_id(0) == 0)
    def _():
        zbuf[...] = jnp.zeros(zbuf.shape, zbuf.dtype)

        def tail_copy(e):
            start = pl.multiple_of(tail_ref[e] * row_words, row_words)
            return pltpu.make_async_copy(zbuf, xs_ref.at[pl.ds(start, EXPERT_BLOCK * row_words)], sem)

        def zstart(e, carry):
            tail_copy(e).start()
            return carry

        def zwait(e, carry):
            tail_copy(e).wait()
            return carry

        lax.fori_loop(0, n_exp, zstart, 0)
        lax.fori_loop(0, n_exp, zwait, 0)

    def row_copy(t, k):
        start = pl.multiple_of(dest_ref[k, t], row_words)
        return pltpu.make_async_copy(h_ref.at[t], xs_ref.at[pl.ds(start, row_words)], sem)

    def start(t, carry):
        for k in range(TOP_K):
            row_copy(t, k).start()
        return carry

    def wait(t, carry):
        for k in range(TOP_K):
            row_copy(t, k).wait()
        return carry

    lax.fori_loop(0, tm, start, 0)
    lax.fori_loop(0, tm, wait, 0)


def _dispatch(tail_start, dest_words, hp2d, n_rows):
    t, row_words = hp2d.shape
    tm = min(t, 256)
    return pl.pallas_call(
        _dispatch_kernel,
        grid_spec=pltpu.PrefetchScalarGridSpec(
            num_scalar_prefetch=1,
            grid=(t // tm,),
            in_specs=[
                pl.BlockSpec((TOP_K, tm), lambda i, tail: (0, i), memory_space=pltpu.SMEM),
                pl.BlockSpec((tm, row_words), lambda i, tail: (i, 0)),
            ],
            out_specs=pl.BlockSpec(memory_space=pl.ANY),
            scratch_shapes=[pltpu.VMEM((EXPERT_BLOCK * row_words,), hp2d.dtype),
                            pltpu.SemaphoreType.DMA(())],
        ),
        out_shape=jax.ShapeDtypeStruct((n_rows * row_words,), hp2d.dtype),
        compiler_params=_cparams("arbitrary"),
        name="dispatch",
    )(tail_start, dest_words, hp2d)


def _expert_kernel(be_ref, nu_ref, xs_hbm, wg_ref, wu_ref, wd_ref, y_hbm, xbuf, ybuf, in_sem, out_sem):
    del be_ref
    i = pl.program_id(0)
    n_used = nu_ref[0]
    n_slab = xbuf.shape[2] // LANES
    slot = lax.rem(i, 2)

    def slabs(hbm, buf, blk, slot_, sem, to_hbm):
        r0 = pl.multiple_of(blk * EXPERT_BLOCK, EXPERT_BLOCK)
        out = []
        for s in range(n_slab):
            far = hbm.at[pl.ds(r0, EXPERT_BLOCK), s, :]
            near = buf.at[slot_, :, pl.ds(s * LANES, LANES)]
            src, dst = (near, far) if to_hbm else (far, near)
            out.append(pltpu.make_async_copy(src, dst, sem.at[slot_]))
        return out

    def load(blk, slot_):
        return slabs(xs_hbm, xbuf, blk, slot_, in_sem, False)

    def store(blk, slot_):
        return slabs(y_hbm, ybuf, blk, slot_, out_sem, True)

    @pl.when(i == 0)
    def _():
        for c in load(0, 0):
            c.start()

    @pl.when(i + 1 < n_used)
    def _():
        for c in load(i + 1, 1 - slot):
            c.start()

    @pl.when(i < n_used)
    def _():
        for c in load(i, slot):
            c.wait()

        @pl.when(i >= 2)
        def _():
            for c in store(i - 2, slot):
                c.wait()

        half = xbuf.shape[2]
        x_lo, x_hi = _unpack_halves(xbuf[slot])
        x_lo = x_lo.astype(BF16)
        x_hi = x_hi.astype(BF16)
        g = (jnp.dot(x_lo, wg_ref[0:half, :], preferred_element_type=F32)
             + jnp.dot(x_hi, wg_ref[half:, :], preferred_element_type=F32))
        u = (jnp.dot(x_lo, wu_ref[0:half, :], preferred_element_type=F32)
             + jnp.dot(x_hi, wu_ref[half:, :], preferred_element_type=F32))
        hid = ((g * _sigmoid(g)) * u).astype(BF16)
        ybuf[slot] = _pack_halves(jnp.dot(hid, wd_ref[...], preferred_element_type=F32))
        for c in store(i, slot):
            c.start()

        @pl.when(i == n_used - 1)
        def _():
            for c in store(i, slot):
                c.wait()

            @pl.when(i >= 1)
            def _():
                for c in store(i - 1, 1 - slot):
                    c.wait()


def _expert_ffn(block_e, n_used, xs, wg_all, wu_all, wd_all, layer):
    n_rows, n_slab, _ = xs.shape
    _, n_exp, d, ff = wg_all.shape
    nb = n_rows // EXPERT_BLOCK
    row_words = n_slab * LANES

    def wsel(i, be, nu):
        return (layer, be[jnp.minimum(i, nu[0] - 1)], 0, 0)

    return pl.pallas_call(
        _expert_kernel,
        grid_spec=pltpu.PrefetchScalarGridSpec(
            num_scalar_prefetch=2,
            grid=(nb,),
            in_specs=[
                pl.BlockSpec(memory_space=pl.ANY),
                pl.BlockSpec((None, None, d, ff), wsel),
                pl.BlockSpec((None, None, d, ff), wsel),
                pl.BlockSpec((None, None, ff, d), wsel),
            ],
            out_specs=pl.BlockSpec(memory_space=pl.ANY),
            scratch_shapes=[
                pltpu.VMEM((2, EXPERT_BLOCK, row_words), jnp.uint32),
                pltpu.VMEM((2, EXPERT_BLOCK, row_words), jnp.uint32),
                pltpu.SemaphoreType.DMA((2,)),
                pltpu.SemaphoreType.DMA((2,)),
            ],
        ),
        out_shape=jax.ShapeDtypeStruct((n_rows, n_slab, LANES), jnp.uint32),
        compiler_params=_cparams("arbitrary"),
        name="expert_ffn",
    )(block_e, n_used, xs, wg_all, wu_all, wd_all)


def _combine_kernel(dest_ref, y_hbm, h_ref, tw_ref, x_ref, mod_ref, sg_ref, su_ref, sd_ref,
                    fg_ref, o_ref, ybuf, sem, *, final_norm):
    tm = h_ref.shape[0]

    row_words = ybuf.shape[2]

    def row_copy(t, k):
        start = pl.multiple_of(dest_ref[k, t], row_words)
        return pltpu.make_async_copy(y_hbm.at[pl.ds(start, row_words)], ybuf.at[k, t], sem)

    def start(t, carry):
        for k in range(TOP_K):
            row_copy(t, k).start()
        return carry

    def wait(t, carry):
        for k in range(TOP_K):
            row_copy(t, k).wait()
        return carry

    lax.fori_loop(0, tm, start, 0)
    hb = h_ref[...].astype(BF16)
    g = jnp.dot(hb, sg_ref[...], preferred_element_type=F32)
    u = jnp.dot(hb, su_ref[...], preferred_element_type=F32)
    acc = jnp.dot(((g * _sigmoid(g)) * u).astype(BF16), sd_ref[...], preferred_element_type=F32)
    lax.fori_loop(0, tm, wait, 0)
    acc_lo = acc[:, :row_words]
    acc_hi = acc[:, row_words:]
    for k in range(TOP_K):
        y_lo, y_hi = _unpack_halves(ybuf[k])
        wk = tw_ref[:, k:k + 1]
        acc_lo = acc_lo + wk * y_lo
        acc_hi = acc_hi + wk * y_hi
    xo = x_ref[...] + mod_ref[5:6, :] * jnp.concatenate([acc_lo, acc_hi], axis=1)
    if final_norm:
        ms = jnp.mean(xo * xo, axis=-1, keepdims=True)
        xo = xo * lax.rsqrt(ms + NORM_EPS) * fg_ref[...]
    o_ref[...] = xo


def _combine(dest, y, h, tw, x, mod, sg_all, su_all, sd_all, layer, fg, final_norm):
    bn, s, d = x.shape
    ff = sg_all.shape[2]
    tm = min(s, 128)
    per_b = s // tm
    blk = pl.BlockSpec((None, tm, d), lambda bi, i: (bi, i, 0))
    return pl.pallas_call(
        functools.partial(_combine_kernel, final_norm=final_norm),
        grid=(bn, per_b),
        in_specs=[
            pl.BlockSpec((TOP_K, tm), lambda bi, i: (0, bi * per_b + i), memory_space=pltpu.SMEM),
            pl.BlockSpec(memory_space=pl.ANY),
            blk,
            pl.BlockSpec((None, tm, TOP_K), lambda bi, i: (bi, i, 0)),
            blk,
            pl.BlockSpec((None, 6, d), lambda bi, i: (bi, 0, 0)),
            pl.BlockSpec((None, d, ff), lambda bi, i: (layer, 0, 0)),
            pl.BlockSpec((None, d, ff), lambda bi, i: (layer, 0, 0)),
            pl.BlockSpec((None, ff, d), lambda bi, i: (layer, 0, 0)),
            pl.BlockSpec((1, d), lambda bi, i: (0, 0)),
        ],
        out_specs=blk,
        out_shape=jax.ShapeDtypeStruct((bn, s, d), F32),
        scratch_shapes=[pltpu.VMEM((TOP_K, tm, d // 2), jnp.uint32), pltpu.SemaphoreType.DMA(())],
        compiler_params=_cparams("arbitrary", "arbitrary"),
        name="combine",
    )(dest, y, h, tw, x, mod, sg_all, su_all, sd_all, fg.reshape(1, d))


def _moe_ffn(h, hp, x, mod, router_w, router_bias, exp_gate, exp_up, exp_down, sh_gate, sh_up, sh_down,
             layer, final_g, final_norm):
    bn, s, d = h.shape
    t = bn * s
    n_exp = router_w.shape[1]
    h2d = h.reshape(t, d)
    top_idx, top_w, rank, counts = _router(h2d, router_w, router_bias)
    cnt = counts.reshape(n_exp).astype(jnp.int32)
    padded = (cnt + EXPERT_BLOCK - 1) // EXPERT_BLOCK * EXPERT_BLOCK
    pends = jnp.cumsum(padded)
    pstarts = pends - padded
    experts = jnp.arange(n_exp, dtype=jnp.int32).reshape(n_exp, 1, 1)
    dest = jnp.sum(jnp.where(top_idx[None] == experts, pstarts.reshape(n_exp, 1, 1), 0), axis=0) + rank
    row_words = d // 2
    dest_words = dest * row_words
    nb = (t * TOP_K) // EXPERT_BLOCK + n_exp
    block_start = jnp.arange(nb, dtype=jnp.int32) * EXPERT_BLOCK
    block_e = jnp.minimum(jnp.sum(pends[None, :] <= block_start[:, None], axis=1), n_exp - 1)
    n_used = (pends[-1] // EXPERT_BLOCK).reshape(1).astype(jnp.int32)
    tail_start = jnp.maximum(pends - EXPERT_BLOCK, 0).astype(jnp.int32)
    n_rows = nb * EXPERT_BLOCK
    n_slab = row_words // LANES
    xs = _dispatch(tail_start, dest_words, hp.reshape(t, row_words), n_rows)
    y = _expert_ffn(block_e.astype(jnp.int32), n_used, xs.reshape(n_rows, n_slab, LANES),
                    exp_gate, exp_up, exp_down, layer)
    tw = jnp.transpose(top_w).reshape(bn, s, TOP_K)
    return _combine(dest_words, y.reshape(n_rows * row_words), h, tw, x, mod, sh_gate, sh_up,
                    sh_down, layer, final_g, final_norm)


def kernel(x, c, ada_w, ada_b, norm_mix_g, w_in, b_in, sgu_ln_g, sgu_ln_b, sgu_w, sgu_b, lru_conv_w, lru_conv_b, lru_wa, lru_ba, lru_wx, lru_bx, lru_lambda, conv_w, conv_b, conv_ln_g, conv_ln_b, w_out, b_out, norm_ffn_g, router_w, router_bias, exp_gate, exp_up, exp_down, sh_gate, sh_up, sh_down, final_norm_g):
    depth = ada_w.shape[0]
    bn, s, d = x.shape
    mod_all = _ada_modulation(c, ada_w, ada_b).reshape(depth, bn, 6, d)
    w_in_b, w_out_b = w_in.astype(BF16), w_out.astype(BF16)
    exp_b = [w.astype(BF16) for w in (exp_gate, exp_up, exp_down)]
    sh_b = [w.astype(BF16) for w in (sh_gate, sh_up, sh_down)]
    for l in range(depth):
        mod = mod_all[l]
        proj = _in_projection(x, mod, norm_mix_g[l], w_in_b, l, b_in[l])
        y = _hybrid_mixer(proj, sgu_ln_g[l], sgu_ln_b[l], sgu_w[l], sgu_b[l], lru_conv_w[l],
                          lru_conv_b[l], lru_wa[l], lru_ba[l], lru_wx[l], lru_bx[l], lru_lambda[l],
                          conv_w[l], conv_b[l], conv_ln_g[l], conv_ln_b[l])
        x, h, hp = _out_projection(y, w_out_b, l, b_out[l], x, mod, norm_ffn_g[l])
        x = _moe_ffn(h, hp, x, mod, router_w[l], router_bias[l], *exp_b, *sh_b, l, final_norm_g,
                     l == depth - 1)
    return x
```

```python
import functools

import jax
import jax.numpy as jnp
from jax import lax
from jax.experimental import pallas as pl
from jax.experimental.pallas import tpu as pltpu

F32 = jnp.float32
BF16 = jnp.bfloat16

LANES = 128
SUBLANES = 8
NORM_EPS = 1e-6
CHUNK = 128
LRU_C = 8.0
LRU_CONV_WIDTH = 4
CONF_CONV_WIDTH = 31
CONV_HALO = 32
LRU_HALO = 8
N_EXPERT_GROUPS = 8
TOPK_GROUPS = 4
TOP_K = 8
ROUTED_SCALE = 2.5
EXPERT_BLOCK = 512
VMEM_LIMIT = 56 * 1024 * 1024


def _cparams(*sem):
    return pltpu.CompilerParams(dimension_semantics=sem, vmem_limit_bytes=VMEM_LIMIT)


def _gelu(x):
    return 0.5 * x * (1.0 + jnp.tanh(0.7978845608028654 * (x + 0.044715 * (x * x * x))))


def _sigmoid(x):
    return 1.0 / (1.0 + jnp.exp(-x))


def _bdot(a, b):
    return jnp.dot(a.astype(BF16), b.astype(BF16), preferred_element_type=F32)


def _ada_kernel(c_ref, w_ref, b_ref, o_ref):
    c = c_ref[...]
    cond = c * _sigmoid(c)
    o_ref[...] = _bdot(cond, w_ref[...]) + b_ref[...]


def _ada_modulation(c, ada_w, ada_b):
    n_layers, d, n_out = ada_w.shape
    bn = c.shape[0]
    tn = 1024
    return pl.pallas_call(
        _ada_kernel,
        grid=(n_layers, n_out // tn),
        in_specs=[
            pl.BlockSpec((bn, d), lambda l, j: (0, 0)),
            pl.BlockSpec((None, d, tn), lambda l, j: (l, 0, j)),
            pl.BlockSpec((None, 1, tn), lambda l, j: (l, 0, j)),
        ],
        out_specs=pl.BlockSpec((None, bn, tn), lambda l, j: (l, 0, j)),
        out_shape=jax.ShapeDtypeStruct((n_layers, bn, n_out), F32),
        compiler_params=_cparams("arbitrary", "arbitrary"),
        name="ada_modulation",
    )(c, ada_w, ada_b.reshape(n_layers, 1, n_out))


def _inproj_kernel(x_ref, mod_ref, g_ref, w_ref, b_ref, o_ref, h_ref):
    @pl.when(pl.program_id(2) == 0)
    def _():
        x = x_ref[...]
        ms = jnp.mean(x * x, axis=-1, keepdims=True)
        y = x * lax.rsqrt(ms + NORM_EPS) * g_ref[...]
        h_ref[...] = (y * (1.0 + mod_ref[1:2, :]) + mod_ref[0:1, :]).astype(BF16)

    acc = jnp.dot(h_ref[...], w_ref[...], preferred_element_type=F32)
    o_ref[...] = (acc + b_ref[...]).astype(o_ref.dtype)


def _in_projection(x, mod, g, w_all, layer, b):
    bn, s, d = x.shape
    n_out = w_all.shape[2]
    tm = min(s, 1024)
    tn = 1024
    return pl.pallas_call(
        _inproj_kernel,
        grid=(bn, s // tm, n_out // tn),
        in_specs=[
            pl.BlockSpec((None, tm, d), lambda bi, i, j: (bi, i, 0)),
            pl.BlockSpec((None, 6, d), lambda bi, i, j: (bi, 0, 0)),
            pl.BlockSpec((1, d), lambda bi, i, j: (0, 0)),
            pl.BlockSpec((None, d, tn), lambda bi, i, j: (layer, 0, j)),
            pl.BlockSpec((1, tn), lambda bi, i, j: (0, j)),
        ],
        out_specs=pl.BlockSpec((None, tm, tn), lambda bi, i, j: (bi, i, j)),
        out_shape=jax.ShapeDtypeStruct((bn, s, n_out), BF16),
        scratch_shapes=[pltpu.VMEM((tm, d), BF16)],
        compiler_params=_cparams("arbitrary", "arbitrary", "arbitrary"),
        name="in_projection",
    )(x, mod, g.reshape(1, d), w_all, b.reshape(1, n_out))


def _linear_scan(a, b, h0):
    n = a.shape[0]
    rows = lax.broadcasted_iota(jnp.int32, a.shape, 0)
    d = 1
    while d < n:
        keep = rows >= d
        b = jnp.where(keep, b + a * pltpu.roll(b, d, 0), b)
        a = jnp.where(keep, a * pltpu.roll(a, d, 0), a)
        d *= 2
    return b + a * h0


def _mixer_kernel(au_ref, av_ref, bx_ref, bg_ref, ca_ref, cb_ref, ga_ref, gb_ref, gc_ref,
                  lng_ref, lnb_ref, sw_ref, sb_ref, lcw_ref, lcb_ref, wa_ref, ba_ref, wx_ref,
                  bxb_ref, lam_ref, cw_ref, cvb_ref, cg_ref, cbt_ref,
                  y_ref, vbuf, cvbuf, gbuf, xbuf, stats, hstate, shbuf):
    ts, width = vbuf.shape
    n_blk = width // LANES

    @pl.when(pl.program_id(1) == 0)
    def _():
        gbuf[0:CONV_HALO, :] = jnp.zeros((CONV_HALO, width), F32)
        xbuf[0:LRU_HALO, :] = jnp.zeros((LRU_HALO, width), F32)
        hstate[...] = jnp.zeros(hstate.shape, F32)

    stats[...] = jnp.zeros(stats.shape, F32)

    def lanes(c):
        return pl.ds(pl.multiple_of(c * LANES, LANES), LANES)

    def pass1(c, carry):
        sl = lanes(c)
        v = _gelu(av_ref[:, sl].astype(F32))
        vbuf[:, sl] = v
        stats[0] += v
        stats[1] += v * v
        glu = ca_ref[:, sl].astype(F32) * _sigmoid(cb_ref[:, sl].astype(F32))
        gbuf[CONV_HALO:CONV_HALO + ts, sl] = glu
        for b in range(1, SUBLANES):
            shbuf[b - 1, SUBLANES:, :] = gbuf[SUBLANES - b:CONV_HALO + ts - b, sl]
        acc = jnp.broadcast_to(cvb_ref[:, sl], (ts, LANES))
        for lag in range(CONF_CONV_WIDTH):
            a, b = divmod(lag, SUBLANES)
            off = CONV_HALO - SUBLANES * a
            tap = gbuf[off:off + ts, sl] if b == 0 else shbuf[b - 1, off:off + ts, :]
            k = CONF_CONV_WIDTH - 1 - lag
            acc = acc + cw_ref[k:k + 1, sl] * tap
        cvbuf[:, sl] = acc
        stats[2] += acc
        stats[3] += acc * acc
        return carry

    lax.fori_loop(0, n_blk, pass1, 0)

    inv_w = 1.0 / width
    for q in (0, 2):
        mean = jnp.sum(stats[q], axis=-1, keepdims=True) * inv_w
        ex2 = jnp.sum(stats[q + 1], axis=-1, keepdims=True) * inv_w
        rstd = lax.rsqrt(ex2 - mean * mean + NORM_EPS)
        stats[q] = jnp.broadcast_to(mean, (ts, LANES))
        stats[q + 1] = jnp.broadcast_to(rstd, (ts, LANES))

    r_i = lax.broadcasted_iota(jnp.int32, (CHUNK, CHUNK), 0)
    c_i = lax.broadcasted_iota(jnp.int32, (CHUNK, CHUNK), 1)
    causal = r_i >= c_i

    def pass2(c, carry):
        sl = lanes(c)
        vn = (vbuf[:, sl] - stats[0]) * stats[1] * lng_ref[:, sl] + lnb_ref[:, sl]
        ws = jnp.where(causal, sw_ref[c], 0.0)
        parts = []
        for n in range(ts // CHUNK):
            parts.append(_bdot(ws, vn[n * CHUNK:(n + 1) * CHUNK, :]) + sb_ref[:, sl])
        mixed = parts[0] if len(parts) == 1 else jnp.concatenate(parts, axis=0)
        y_a = _gelu(au_ref[:, sl].astype(F32)) * mixed
        xbuf[LRU_HALO:LRU_HALO + ts, sl] = bx_ref[:, sl].astype(F32)
        xb = jnp.broadcast_to(lcb_ref[:, sl], (ts, LANES))
        for k in range(LRU_CONV_WIDTH):
            off = LRU_HALO - (LRU_CONV_WIDTH - 1) + k
            xb = xb + lcw_ref[k:k + 1, sl] * xbuf[off:off + ts, sl]
        r = _sigmoid(_bdot(xb, wa_ref[c]) + ba_ref[:, sl])
        ig = _sigmoid(_bdot(xb, wx_ref[c]) + bxb_ref[:, sl])
        lam = lam_ref[:, sl]
        softplus = jnp.maximum(-lam, 0.0) + jnp.log1p(jnp.exp(-jnp.abs(lam)))
        log_a = (-LRU_C) * r * softplus
        a = jnp.exp(log_a)
        bterm = jnp.sqrt(-jnp.tanh(log_a) * (a * a + 1.0)) * (ig * xb)
        h = _linear_scan(a, bterm, hstate[0:1, sl])
        hstate[0:1, sl] = h[ts - 1:ts, :]
        y_b = _gelu(bg_ref[:, sl].astype(F32)) * h
        z = (cvbuf[:, sl] - stats[2]) * stats[3] * cg_ref[:, sl] + cbt_ref[:, sl]
        y_c = z * _sigmoid(z)
        y = (_sigmoid(ga_ref[:, sl].astype(F32)) * y_a + _sigmoid(gb_ref[:, sl].astype(F32)) * y_b
             + _sigmoid(gc_ref[:, sl].astype(F32)) * y_c)
        y_ref[:, sl] = y.astype(y_ref.dtype)
        return carry

    lax.fori_loop(0, n_blk, pass2, 0)

    gbuf[0:CONV_HALO, :] = gbuf[ts:ts + CONV_HALO, :]
    xbuf[0:LRU_HALO, :] = xbuf[ts:ts + LRU_HALO, :]


def _hybrid_mixer(proj, sgu_ln_g, sgu_ln_b, sgu_w, sgu_b, lru_conv_w, lru_conv_b, lru_wa, lru_ba,
                  lru_wx, lru_bx, lru_lambda, conv_w, conv_b, conv_ln_g, conv_ln_b):
    bn, s, n9 = proj.shape
    width = n9 // 9
    n_grp = width // LANES
    assert sgu_w.shape == (n_grp, CHUNK, CHUNK) and lru_wa.shape == (n_grp, LANES, LANES)
    ts = CHUNK
    row = lambda v: v.reshape(1, width)
    sb = jnp.repeat(jnp.transpose(sgu_b), LANES, axis=1)
    cw = jnp.pad(conv_w, ((0, CONV_HALO - CONF_CONV_WIDTH), (0, 0)))
    lcw = jnp.pad(lru_conv_w, ((0, LRU_HALO - LRU_CONV_WIDTH), (0, 0)))

    def split(k):
        return pl.BlockSpec((None, ts, width), lambda bi, i, k=k: (bi, i, k))

    def full(shape):
        return pl.BlockSpec(shape, lambda bi, i: (0,) * len(shape))

    vec = full((1, width))
    grp = full((n_grp, LANES, LANES))
    return pl.pallas_call(
        _mixer_kernel,
        grid=(bn, s // ts),
        in_specs=[split(k) for k in range(9)] + [
            vec, vec, grp, full((CHUNK, width)), full((LRU_HALO, width)), vec, grp, vec, grp, vec,
            vec, full((CONV_HALO, width)), vec, vec, vec],
        out_specs=pl.BlockSpec((None, ts, width), lambda bi, i: (bi, i, 0)),
        out_shape=jax.ShapeDtypeStruct((bn, s, width), BF16),
        scratch_shapes=[
            pltpu.VMEM((ts, width), F32),
            pltpu.VMEM((ts, width), F32),
            pltpu.VMEM((CONV_HALO + ts, width), F32),
            pltpu.VMEM((LRU_HALO + ts, width), F32),
            pltpu.VMEM((4, ts, LANES), F32),
            pltpu.VMEM((8, width), F32),
            pltpu.VMEM((SUBLANES - 1, CONV_HALO + ts, LANES), F32),
        ],
        compiler_params=_cparams("arbitrary", "arbitrary"),
        name="hybrid_mixer",
    )(*([proj] * 9), row(sgu_ln_g), row(sgu_ln_b), sgu_w, sb, lcw, row(lru_conv_b), lru_wa,
      row(lru_ba), lru_wx, row(lru_bx), row(lru_lambda), cw, row(conv_b), row(conv_ln_g),
      row(conv_ln_b))


HI_MASK = 0xFFFF0000


def _pack_halves(v):
    n = v.shape[-1] // 2
    lo = lax.bitcast_convert_type(v[:, :n].astype(BF16).astype(F32), jnp.uint32)
    hi = lax.bitcast_convert_type(v[:, n:].astype(BF16).astype(F32), jnp.uint32)
    return (lo >> jnp.uint32(16)) | (hi & jnp.uint32(HI_MASK))


def _unpack_halves(p):
    lo = lax.bitcast_convert_type(p << jnp.uint32(16), F32)
    hi = lax.bitcast_convert_type(p & jnp.uint32(HI_MASK), F32)
    return lo, hi


def _outproj_kernel(y_ref, w_ref, b_ref, x_ref, mod_ref, g_ref, xo_ref, h_ref, hp_ref):
    mix = jnp.dot(y_ref[...], w_ref[...], preferred_element_type=F32) + b_ref[...]
    xn = x_ref[...] + mod_ref[2:3, :] * mix
    xo_ref[...] = xn
    ms = jnp.mean(xn * xn, axis=-1, keepdims=True)
    h = xn * lax.rsqrt(ms + NORM_EPS) * g_ref[...]
    h = h * (1.0 + mod_ref[4:5, :]) + mod_ref[3:4, :]
    h_ref[...] = h
    hp_ref[...] = _pack_halves(h)


def _out_projection(y, w_all, layer, b, x, mod, g):
    bn, s, d = x.shape
    width = y.shape[-1]
    tm = min(s, 256)
    blk = pl.BlockSpec((None, tm, d), lambda bi, i: (bi, i, 0))
    pblk = pl.BlockSpec((None, tm, d // 2), lambda bi, i: (bi, i, 0))
    return pl.pallas_call(
        _outproj_kernel,
        grid=(bn, s // tm),
        in_specs=[
            pl.BlockSpec((None, tm, width), lambda bi, i: (bi, i, 0)),
            pl.BlockSpec((None, width, d), lambda bi, i: (layer, 0, 0)),
            pl.BlockSpec((1, d), lambda bi, i: (0, 0)),
            blk,
            pl.BlockSpec((None, 6, d), lambda bi, i: (bi, 0, 0)),
            pl.BlockSpec((1, d), lambda bi, i: (0, 0)),
        ],
        out_specs=[blk, blk, pblk],
        out_shape=[jax.ShapeDtypeStruct((bn, s, d), F32), jax.ShapeDtypeStruct((bn, s, d), F32),
                   jax.ShapeDtypeStruct((bn, s, d // 2), jnp.uint32)],
        compiler_params=_cparams("arbitrary", "arbitrary"),
        name="out_projection",
    )(y, w_all, b.reshape(1, d), x, mod, g.reshape(1, d))


def _router_kernel(h_ref, rw_ref, rb_ref, idx_ref, w_ref, rank_ref, cnt_ref, carry_ref):
    n_exp = rw_ref.shape[0]
    tm = h_ref.shape[0]
    per_grp = n_exp // N_EXPERT_GROUPS
    neg = -jnp.inf

    @pl.when(pl.program_id(0) == 0)
    def _():
        carry_ref[...] = jnp.zeros(carry_ref.shape, F32)

    logits = lax.dot_general(rw_ref[...], h_ref[...], (((1,), (1,)), ((), ())),
                             precision=lax.Precision.HIGHEST, preferred_element_type=F32)
    scores = _sigmoid(logits)
    sel = (scores + rb_ref[...]).reshape(N_EXPERT_GROUPS, per_grp, tm)
    scores3 = scores.reshape(N_EXPERT_GROUPS, per_grp, tm)
    shape3 = (N_EXPERT_GROUPS, per_grp, tm)
    sub = lax.broadcasted_iota(jnp.int32, shape3, 1)
    gid = lax.broadcasted_iota(jnp.int32, shape3, 0)
    lin = gid * per_grp + sub

    m1 = jnp.max(sel, axis=1, keepdims=True)
    i1 = jnp.min(jnp.where(sel == m1, sub, per_grp), axis=1, keepdims=True)
    m2 = jnp.max(jnp.where(sub == i1, neg, sel), axis=1, keepdims=True)
    gscore = m1 + m2
    gidx = lax.broadcasted_iota(jnp.int32, gscore.shape, 0)
    gmask = jnp.zeros(gscore.shape, jnp.bool_)
    cur = gscore
    for _ in range(TOPK_GROUPS):
        m = jnp.max(cur, axis=0, keepdims=True)
        pick = gidx == jnp.min(jnp.where(cur == m, gidx, N_EXPERT_GROUPS), axis=0, keepdims=True)
        gmask = jnp.logical_or(gmask, pick)
        cur = jnp.where(pick, neg, cur)

    cur = jnp.where(gmask, sel, neg)
    chosen = jnp.zeros(shape3, F32)
    idx_rows, score_rows = [], []
    for _ in range(TOP_K):
        m = jnp.max(jnp.max(cur, axis=1, keepdims=True), axis=0, keepdims=True)
        first = jnp.min(jnp.min(jnp.where(cur == m, lin, n_exp), axis=1, keepdims=True),
                        axis=0, keepdims=True)
        pick = lin == first
        chosen = jnp.where(pick, 1.0, chosen)
        cur = jnp.where(pick, neg, cur)
        sc = jnp.sum(jnp.sum(jnp.where(pick, scores3, 0.0), axis=1, keepdims=True), axis=0,
                     keepdims=True)
        idx_rows.append(first.reshape(1, tm))
        score_rows.append(sc.reshape(1, tm))

    chosen2 = chosen.reshape(n_exp, tm)
    s_i = lax.broadcasted_iota(jnp.int32, (tm, tm), 0)
    t_i = lax.broadcasted_iota(jnp.int32, (tm, tm), 1)
    before = jnp.where(s_i < t_i, 1.0, 0.0).astype(BF16)
    prefix = jnp.dot(chosen2.astype(BF16), before, preferred_element_type=F32)
    rank3 = (prefix + carry_ref[...]).reshape(shape3)
    carry_ref[...] = carry_ref[...] + jnp.sum(chosen2, axis=1, keepdims=True)
    cnt_ref[...] = carry_ref[...]

    rank_rows = []
    for k in range(TOP_K):
        pick = lin == idx_rows[k].reshape(1, 1, tm)
        rk = jnp.sum(jnp.sum(jnp.where(pick, rank3, 0.0), axis=1, keepdims=True), axis=0,
                     keepdims=True)
        rank_rows.append(rk.reshape(1, tm))

    top_s = jnp.concatenate(score_rows, axis=0)
    idx_ref[...] = jnp.concatenate(idx_rows, axis=0)
    w_ref[...] = ROUTED_SCALE * top_s / jnp.sum(top_s, axis=0, keepdims=True)
    rank_ref[...] = jnp.concatenate(rank_rows, axis=0).astype(jnp.int32)


def _router(h2d, router_w, router_bias):
    t, d = h2d.shape
    n_exp = router_w.shape[1]
    tm = min(t, 256)
    row = pl.BlockSpec((TOP_K, tm), lambda i: (0, i))
    return pl.pallas_call(
        _router_kernel,
        grid=(t // tm,),
        in_specs=[
            pl.BlockSpec((tm, d), lambda i: (i, 0)),
            pl.BlockSpec((n_exp, d), lambda i: (0, 0)),
            pl.BlockSpec((n_exp, 1), lambda i: (0, 0)),
        ],
        out_specs=[row, row, row, pl.BlockSpec((n_exp, 1), lambda i: (0, 0))],
        out_shape=[
            jax.ShapeDtypeStruct((TOP_K, t), jnp.int32),
            jax.ShapeDtypeStruct((TOP_K, t), F32),
            jax.ShapeDtypeStruct((TOP_K, t), jnp.int32),
            jax.ShapeDtypeStruct((n_exp, 1), F32),
        ],
        scratch_shapes=[pltpu.VMEM((n_exp, 1), F32)],
        compiler_params=_cparams("arbitrary"),
        name="router",
    )(h2d, jnp.transpose(router_w), router_bias.reshape(n_exp, 1))


def _dispatch_kernel(tail_ref, dest_ref, h_ref, xs_ref, zbuf, sem):
    tm, row_words = h_ref.shape
    n_exp = tail_ref.shape[0]

    @pl.when(pl.program_id(0) == 0)
    def _():
        zbuf[...] = jnp.zeros(zbuf.shape, zbuf.dtype)

        def tail_copy(e):
            start = pl.multiple_of(tail_ref[e] * row_words, row_words)
            return pltpu.make_async_copy(zbuf, xs_ref.at[pl.ds(start, EXPERT_BLOCK * row_words)], sem)

        def zstart(e, carry):
            tail_copy(e).start()
            return carry

        def zwait(e, carry):
            tail_copy(e).wait()
            return carry

        lax.fori_loop(0, n_exp, zstart, 0)
        lax.fori_loop(0, n_exp, zwait, 0)

    def row_copy(t, k):
        start = pl.multiple_of(dest_ref[k, t], row_words)
        return pltpu.make_async_copy(h_ref.at[t], xs_ref.at[pl.ds(start, row_words)], sem)

    def start(t, carry):
        for k in range(TOP_K):
            row_copy(t, k).start()
        return carry

    def wait(t, carry):
        for k in range(TOP_K):
            row_copy(t, k).wait()
        return carry

    lax.fori_loop(0, tm, start, 0)
    lax.fori_loop(0, tm, wait, 0)


def _dispatch(tail_start, dest_words, hp2d, n_rows):
    t, row_words = hp2d.shape
    tm = min(t, 256)
    return pl.pallas_call(
        _dispatch_kernel,
        grid_spec=pltpu.PrefetchScalarGridSpec(
            num_scalar_prefetch=1,
            grid=(t // tm,),
            in_specs=[
                pl.BlockSpec((TOP_K, tm), lambda i, tail: (0, i), memory_space=pltpu.SMEM),
                pl.BlockSpec((tm, row_words), lambda i, tail: (i, 0)),
            ],
            out_specs=pl.BlockSpec(memory_space=pl.ANY),
            scratch_shapes=[pltpu.VMEM((EXPERT_BLOCK * row_words,), hp2d.dtype),
                            pltpu.SemaphoreType.DMA(())],
        ),
        out_shape=jax.ShapeDtypeStruct((n_rows * row_words,), hp2d.dtype),
        compiler_params=_cparams("arbitrary"),
        name="dispatch",
    )(tail_start, dest_words, hp2d)


def _expert_kernel(be_ref, nu_ref, xs_hbm, wg_ref, wu_ref, wd_ref, y_hbm, xbuf, ybuf, wgb, wub, wdb,
                   in_sem, out_sem):
    i = pl.program_id(0)
    n_used = nu_ref[0]

    new_expert = jnp.logical_or(i == 0, be_ref[i] != be_ref[jnp.maximum(i - 1, 0)])

    @pl.when(jnp.logical_and(i < n_used, new_expert))
    def _():
        wgb[...] = wg_ref[...].astype(BF16)
        wub[...] = wu_ref[...].astype(BF16)
        wdb[...] = wd_ref[...].astype(BF16)

    n_slab = xbuf.shape[2] // LANES
    slot = lax.rem(i, 2)

    def slabs(hbm, buf, blk, slot_, sem, to_hbm):
        r0 = pl.multiple_of(blk * EXPERT_BLOCK, EXPERT_BLOCK)
        out = []
        for s in range(n_slab):
            far = hbm.at[pl.ds(r0, EXPERT_BLOCK), s, :]
            near = buf.at[slot_, :, pl.ds(s * LANES, LANES)]
            src, dst = (near, far) if to_hbm else (far, near)
            out.append(pltpu.make_async_copy(src, dst, sem.at[slot_]))
        return out

    def load(blk, slot_):
        return slabs(xs_hbm, xbuf, blk, slot_, in_sem, False)

    def store(blk, slot_):
        return slabs(y_hbm, ybuf, blk, slot_, out_sem, True)

    @pl.when(i == 0)
    def _():
        for c in load(0, 0):
            c.start()

    @pl.when(i + 1 < n_used)
    def _():
        for c in load(i + 1, 1 - slot):
            c.start()

    @pl.when(i < n_used)
    def _():
        for c in load(i, slot):
            c.wait()

        @pl.when(i >= 2)
        def _():
            for c in store(i - 2, slot):
                c.wait()

        half = xbuf.shape[2]
        x_lo, x_hi = _unpack_halves(xbuf[slot])
        x_lo = x_lo.astype(BF16)
        x_hi = x_hi.astype(BF16)
        g = (jnp.dot(x_lo, wgb[0:half, :], preferred_element_type=F32)
             + jnp.dot(x_hi, wgb[half:, :], preferred_element_type=F32))
        u = (jnp.dot(x_lo, wub[0:half, :], preferred_element_type=F32)
             + jnp.dot(x_hi, wub[half:, :], preferred_element_type=F32))
        hid = ((g * _sigmoid(g)) * u).astype(BF16)
        ybuf[slot] = _pack_halves(jnp.dot(hid, wdb[...], preferred_element_type=F32))
        for c in store(i, slot):
            c.start()

        @pl.when(i == n_used - 1)
        def _():
            for c in store(i, slot):
                c.wait()

            @pl.when(i >= 1)
            def _():
                for c in store(i - 1, 1 - slot):
                    c.wait()


def _expert_ffn(block_e, n_used, xs, wg_all, wu_all, wd_all, layer):
    n_rows, n_slab, _ = xs.shape
    _, n_exp, d, ff = wg_all.shape
    nb = n_rows // EXPERT_BLOCK
    row_words = n_slab * LANES

    def wsel(i, be, nu):
        return (layer, be[jnp.minimum(i, nu[0] - 1)], 0, 0)

    return pl.pallas_call(
        _expert_kernel,
        grid_spec=pltpu.PrefetchScalarGridSpec(
            num_scalar_prefetch=2,
            grid=(nb,),
            in_specs=[
                pl.BlockSpec(memory_space=pl.ANY),
                pl.BlockSpec((None, None, d, ff), wsel),
                pl.BlockSpec((None, None, d, ff), wsel),
                pl.BlockSpec((None, None, ff, d), wsel),
            ],
            out_specs=pl.BlockSpec(memory_space=pl.ANY),
            scratch_shapes=[
                pltpu.VMEM((2, EXPERT_BLOCK, row_words), jnp.uint32),
                pltpu.VMEM((2, EXPERT_BLOCK, row_words), jnp.uint32),
                pltpu.VMEM((d, ff), BF16),
                pltpu.VMEM((d, ff), BF16),
                pltpu.VMEM((ff, d), BF16),
                pltpu.SemaphoreType.DMA((2,)),
                pltpu.SemaphoreType.DMA((2,)),
            ],
        ),
        out_shape=jax.ShapeDtypeStruct((n_rows, n_slab, LANES), jnp.uint32),
        compiler_params=_cparams("arbitrary"),
        name="expert_ffn",
    )(block_e, n_used, xs, wg_all, wu_all, wd_all)


COMBINE_TILE = 128


def _combine_kernel(d_first_ref, d_b_ref, d_next_ref, y_hbm, h_ref, tw_ref, x_ref, mod_ref, sg_ref,
                    su_ref, sd_ref, fg_ref, o_ref, ybuf_a, ybuf_b, sem_a, sem_b, *, final_norm):
    j = pl.program_id(0) * pl.num_programs(1) + pl.program_id(1)
    last = pl.num_programs(0) * pl.num_programs(1) - 1
    tile, row_words = ybuf_a.shape[1], ybuf_a.shape[2]

    def row_copy(d_ref, buf, sem, t, k):
        start = pl.multiple_of(d_ref[k, t], row_words)
        return pltpu.make_async_copy(y_hbm.at[pl.ds(start, row_words)], buf.at[k, t], sem)

    def start_all(d_ref, buf, sem):
        for t in range(tile):
            for k in range(TOP_K):
                row_copy(d_ref, buf, sem, t, k).start()

    def wait_all(d_ref, buf, sem):
        for t in range(tile):
            for k in range(TOP_K):
                row_copy(d_ref, buf, sem, 0, 0).wait()

    def shared_ffn(rows):
        hb = h_ref[rows, :].astype(BF16)
        g = jnp.dot(hb, sg_ref[...], preferred_element_type=F32)
        u = jnp.dot(hb, su_ref[...], preferred_element_type=F32)
        return jnp.dot(((g * _sigmoid(g)) * u).astype(BF16), sd_ref[...], preferred_element_type=F32)

    def finish(rows, acc, buf):
        acc_lo = acc[:, :row_words]
        acc_hi = acc[:, row_words:]
        for k in range(TOP_K):
            y_lo, y_hi = _unpack_halves(buf[k])
            wk = tw_ref[rows, k:k + 1]
            acc_lo = acc_lo + wk * y_lo
            acc_hi = acc_hi + wk * y_hi
        xo = x_ref[rows, :] + mod_ref[5:6, :] * jnp.concatenate([acc_lo, acc_hi], axis=1)
        if final_norm:
            ms = jnp.mean(xo * xo, axis=-1, keepdims=True)
            xo = xo * lax.rsqrt(ms + NORM_EPS) * fg_ref[...]
        o_ref[rows, :] = xo

    rows_a = slice(0, tile)
    rows_b = slice(tile, 2 * tile)

    @pl.when(j == 0)
    def _():
        def first(t, carry):
            for k in range(TOP_K):
                row_copy(d_first_ref, ybuf_a, sem_a, t, k).start()
            return carry
        lax.fori_loop(0, tile, first, 0)

    start_all(d_b_ref, ybuf_b, sem_b)
    acc_a = shared_ffn(rows_a)
    wait_all(d_b_ref, ybuf_a, sem_a)
    finish(rows_a, acc_a, ybuf_a)
    acc_b = shared_ffn(rows_b)
    wait_all(d_b_ref, ybuf_b, sem_b)
    start_all(d_next_ref, ybuf_a, sem_a)
    finish(rows_b, acc_b, ybuf_b)

    @pl.when(j == last)
    def _():
        wait_all(d_next_ref, ybuf_a, sem_a)


def _combine(dest, y, h, tw, x, mod, sg_all, su_all, sd_all, layer, fg, final_norm):
    bn, s, d = x.shape
    ff = sg_all.shape[2]
    tile = COMBINE_TILE
    tm = 2 * tile
    assert s % tm == 0
    per_b = s // tm
    n_tiles = bn * s // tile
    blk = pl.BlockSpec((None, tm, d), lambda bi, i: (bi, i, 0))

    def smem_tile(index):
        return pl.BlockSpec((TOP_K, tile), lambda bi, i: (0, index(bi * per_b + i)),
                            memory_space=pltpu.SMEM)

    return pl.pallas_call(
        functools.partial(_combine_kernel, final_norm=final_norm),
        grid=(bn, per_b),
        in_specs=[
            smem_tile(lambda j: 0),
            smem_tile(lambda j: 2 * j + 1),
            smem_tile(lambda j: jnp.minimum(2 * j + 2, n_tiles - 1)),
            pl.BlockSpec(memory_space=pl.ANY),
            blk,
            pl.BlockSpec((None, tm, TOP_K), lambda bi, i: (bi, i, 0)),
            blk,
            pl.BlockSpec((None, 6, d), lambda bi, i: (bi, 0, 0)),
            pl.BlockSpec((None, d, ff), lambda bi, i: (layer, 0, 0)),
            pl.BlockSpec((None, d, ff), lambda bi, i: (layer, 0, 0)),
            pl.BlockSpec((None, ff, d), lambda bi, i: (layer, 0, 0)),
            pl.BlockSpec((1, d), lambda bi, i: (0, 0)),
        ],
        out_specs=blk,
        out_shape=jax.ShapeDtypeStruct((bn, s, d), F32),
        scratch_shapes=[pltpu.VMEM((TOP_K, tile, d // 2), jnp.uint32),
                        pltpu.VMEM((TOP_K, tile, d // 2), jnp.uint32),
                        pltpu.SemaphoreType.DMA(()), pltpu.SemaphoreType.DMA(())],
        compiler_params=_cparams("arbitrary", "arbitrary"),
        name="combine",
    )(dest, dest, dest, y, h, tw, x, mod, sg_all, su_all, sd_all, fg.reshape(1, d))


def _moe_ffn(h, hp, x, mod, router_w, router_bias, exp_gate, exp_up, exp_down, sh_gate, sh_up, sh_down,
             layer, final_g, final_norm):
    bn, s, d = h.shape
    t = bn * s
    n_exp = router_w.shape[1]
    h2d = h.reshape(t, d)
    top_idx, top_w, rank, counts = _router(h2d, router_w, router_bias)
    cnt = counts.reshape(n_exp).astype(jnp.int32)
    padded = (cnt + EXPERT_BLOCK - 1) // EXPERT_BLOCK * EXPERT_BLOCK
    pends = jnp.cumsum(padded)
    pstarts = pends - padded
    experts = jnp.arange(n_exp, dtype=jnp.int32).reshape(n_exp, 1, 1)
    dest = jnp.sum(jnp.where(top_idx[None] == experts, pstarts.reshape(n_exp, 1, 1), 0), axis=0) + rank
    row_words = d // 2
    dest_words = dest * row_words
    nb = (t * TOP_K) // EXPERT_BLOCK + n_exp
    block_start = jnp.arange(nb, dtype=jnp.int32) * EXPERT_BLOCK
    block_e = jnp.minimum(jnp.sum(pends[None, :] <= block_start[:, None], axis=1), n_exp - 1)
    n_used = (pends[-1] // EXPERT_BLOCK).reshape(1).astype(jnp.int32)
    tail_start = jnp.maximum(pends - EXPERT_BLOCK, 0).astype(jnp.int32)
    n_rows = nb * EXPERT_BLOCK
    n_slab = row_words // LANES
    xs = _dispatch(tail_start, dest_words, hp.reshape(t, row_words), n_rows)
    y = _expert_ffn(block_e.astype(jnp.int32), n_used, xs.reshape(n_rows, n_slab, LANES),
                    exp_gate, exp_up, exp_down, layer)
    tw = jnp.transpose(top_w).reshape(bn, s, TOP_K)
    return _combine(dest_words, y.reshape(n_rows * row_words), h, tw, x, mod, sh_gate, sh_up,
                    sh_down, layer, final_g, final_norm)


def kernel(x, c, ada_w, ada_b, norm_mix_g, w_in, b_in, sgu_ln_g, sgu_ln_b, sgu_w, sgu_b, lru_conv_w, lru_conv_b, lru_wa, lru_ba, lru_wx, lru_bx, lru_lambda, conv_w, conv_b, conv_ln_g, conv_ln_b, w_out, b_out, norm_ffn_g, router_w, router_bias, exp_gate, exp_up, exp_down, sh_gate, sh_up, sh_down, final_norm_g):
    depth = ada_w.shape[0]
    bn, s, d = x.shape
    mod_all = _ada_modulation(c, ada_w, ada_b).reshape(depth, bn, 6, d)
    w_in_b, w_out_b = w_in.astype(BF16), w_out.astype(BF16)
    exp_b = [exp_gate, exp_up, exp_down]
    sh_b = [w.astype(BF16) for w in (sh_gate, sh_up, sh_down)]
    for l in range(depth):
        mod = mod_all[l]
        proj = _in_projection(x, mod, norm_mix_g[l], w_in_b, l, b_in[l])
        y = _hybrid_mixer(proj, sgu_ln_g[l], sgu_ln_b[l], sgu_w[l], sgu_b[l], lru_conv_w[l],
                          lru_conv_b[l], lru_wa[l], lru_ba[l], lru_wx[l], lru_bx[l], lru_lambda[l],
                          conv_w[l], conv_b[l], conv_ln_g[l], conv_ln_b[l])
        x, h, hp = _out_projection(y, w_out_b, l, b_out[l], x, mod, norm_ffn_g[l])
        x = _moe_ffn(h, hp, x, mod, router_w[l], router_bias[l], *exp_b, *sh_b, l, final_norm_g,
                     l == depth - 1)
    return x
```

```python
import functools

import jax
import jax.numpy as jnp
from jax import lax
from jax.experimental import pallas as pl
from jax.experimental.pallas import tpu as pltpu

F32 = jnp.float32
BF16 = jnp.bfloat16

LANES = 128
SUBLANES = 8
NORM_EPS = 1e-6
CHUNK = 128
LRU_C = 8.0
LRU_CONV_WIDTH = 4
CONF_CONV_WIDTH = 31
CONV_HALO = 32
LRU_HALO = 8
N_EXPERT_GROUPS = 8
TOPK_GROUPS = 4
TOP_K = 8
ROUTED_SCALE = 2.5
EXPERT_BLOCK = 512
VMEM_LIMIT = 56 * 1024 * 1024


def _cparams(*sem):
    return pltpu.CompilerParams(dimension_semantics=sem, vmem_limit_bytes=VMEM_LIMIT)


def _gelu(x):
    return 0.5 * x * (1.0 + jnp.tanh(0.7978845608028654 * (x + 0.044715 * (x * x * x))))


def _sigmoid(x):
    return 1.0 / (1.0 + jnp.exp(-x))


def _bdot(a, b):
    return jnp.dot(a.astype(BF16), b.astype(BF16), preferred_element_type=F32)


def _ada_kernel(c_ref, w_ref, b_ref, o_ref):
    c = c_ref[...]
    cond = c * _sigmoid(c)
    o_ref[...] = _bdot(cond, w_ref[...]) + b_ref[...]


def _ada_modulation(c, ada_w, ada_b):
    n_layers, d, n_out = ada_w.shape
    bn = c.shape[0]
    tn = 1024
    return pl.pallas_call(
        _ada_kernel,
        grid=(n_layers, n_out // tn),
        in_specs=[
            pl.BlockSpec((bn, d), lambda l, j: (0, 0)),
            pl.BlockSpec((None, d, tn), lambda l, j: (l, 0, j)),
            pl.BlockSpec((None, 1, tn), lambda l, j: (l, 0, j)),
        ],
        out_specs=pl.BlockSpec((None, bn, tn), lambda l, j: (l, 0, j)),
        out_shape=jax.ShapeDtypeStruct((n_layers, bn, n_out), F32),
        compiler_params=_cparams("arbitrary", "arbitrary"),
        name="ada_modulation",
    )(c, ada_w, ada_b.reshape(n_layers, 1, n_out))


def _inproj_kernel(x_ref, mod_ref, g_ref, w_ref, b_ref, o_ref, h_ref):
    @pl.when(pl.program_id(2) == 0)
    def _():
        x = x_ref[...]
        ms = jnp.mean(x * x, axis=-1, keepdims=True)
        y = x * lax.rsqrt(ms + NORM_EPS) * g_ref[...]
        h_ref[...] = (y * (1.0 + mod_ref[1:2, :]) + mod_ref[0:1, :]).astype(BF16)

    acc = jnp.dot(h_ref[...], w_ref[...], preferred_element_type=F32)
    o_ref[...] = (acc + b_ref[...]).astype(o_ref.dtype)


def _in_projection(x, mod, g, w_all, layer, b):
    bn, s, d = x.shape
    n_out = w_all.shape[2]
    tm = min(s, 1024)
    tn = 2048
    return pl.pallas_call(
        _inproj_kernel,
        grid=(bn, s // tm, n_out // tn),
        in_specs=[
            pl.BlockSpec((None, tm, d), lambda bi, i, j: (bi, i, 0)),
            pl.BlockSpec((None, 6, d), lambda bi, i, j: (bi, 0, 0)),
            pl.BlockSpec((1, d), lambda bi, i, j: (0, 0)),
            pl.BlockSpec((None, d, tn), lambda bi, i, j: (layer, 0, j)),
            pl.BlockSpec((1, tn), lambda bi, i, j: (0, j)),
        ],
        out_specs=pl.BlockSpec((None, tm, tn), lambda bi, i, j: (bi, i, j)),
        out_shape=jax.ShapeDtypeStruct((bn, s, n_out), BF16),
        scratch_shapes=[pltpu.VMEM((tm, d), BF16)],
        compiler_params=_cparams("arbitrary", "arbitrary", "arbitrary"),
        name="in_projection",
    )(x, mod, g.reshape(1, d), w_all, b.reshape(1, n_out))


def _linear_scan(a, b, h0):
    n = a.shape[0]
    rows = lax.broadcasted_iota(jnp.int32, a.shape, 0)
    d = 1
    while d < n:
        keep = rows >= d
        b = jnp.where(keep, b + a * pltpu.roll(b, d, 0), b)
        a = jnp.where(keep, a * pltpu.roll(a, d, 0), a)
        d *= 2
    return b + a * h0


def _mixer_kernel(au_ref, av_ref, bx_ref, bg_ref, ca_ref, cb_ref, ga_ref, gb_ref, gc_ref,
                  lng_ref, lnb_ref, sw_ref, sb_ref, lcw_ref, lcb_ref, wa_ref, ba_ref, wx_ref,
                  bxb_ref, lam_ref, cw_ref, cvb_ref, cg_ref, cbt_ref,
                  y_ref, vbuf, cvbuf, gbuf, xbuf, stats, hstate, shbuf):
    ts, width = vbuf.shape
    n_blk = width // LANES

    @pl.when(pl.program_id(1) == 0)
    def _():
        gbuf[0:CONV_HALO, :] = jnp.zeros((CONV_HALO, width), F32)
        xbuf[0:LRU_HALO, :] = jnp.zeros((LRU_HALO, width), F32)
        hstate[...] = jnp.zeros(hstate.shape, F32)

    stats[...] = jnp.zeros(stats.shape, F32)

    def lanes(c):
        return pl.ds(pl.multiple_of(c * LANES, LANES), LANES)

    def pass1(c, carry):
        sl = lanes(c)
        v = _gelu(av_ref[:, sl].astype(F32))
        vbuf[:, sl] = v
        stats[0] += v
        stats[1] += v * v
        glu = ca_ref[:, sl].astype(F32) * _sigmoid(cb_ref[:, sl].astype(F32))
        gbuf[CONV_HALO:CONV_HALO + ts, sl] = glu
        for b in range(1, SUBLANES):
            shbuf[b - 1, SUBLANES:, :] = gbuf[SUBLANES - b:CONV_HALO + ts - b, sl]
        acc = jnp.broadcast_to(cvb_ref[:, sl], (ts, LANES))
        for lag in range(CONF_CONV_WIDTH):
            a, b = divmod(lag, SUBLANES)
            off = CONV_HALO - SUBLANES * a
            tap = gbuf[off:off + ts, sl] if b == 0 else shbuf[b - 1, off:off + ts, :]
            k = CONF_CONV_WIDTH - 1 - lag
            acc = acc + cw_ref[k:k + 1, sl] * tap
        cvbuf[:, sl] = acc
        stats[2] += acc
        stats[3] += acc * acc
        return carry

    lax.fori_loop(0, n_blk, pass1, 0)

    inv_w = 1.0 / width
    for q in (0, 2):
        mean = jnp.sum(stats[q], axis=-1, keepdims=True) * inv_w
        ex2 = jnp.sum(stats[q + 1], axis=-1, keepdims=True) * inv_w
        rstd = lax.rsqrt(ex2 - mean * mean + NORM_EPS)
        stats[q] = jnp.broadcast_to(mean, (ts, LANES))
        stats[q + 1] = jnp.broadcast_to(rstd, (ts, LANES))

    r_i = lax.broadcasted_iota(jnp.int32, (CHUNK, CHUNK), 0)
    c_i = lax.broadcasted_iota(jnp.int32, (CHUNK, CHUNK), 1)
    causal = r_i >= c_i

    def pass2(c, carry):
        sl = lanes(c)
        vn = (vbuf[:, sl] - stats[0]) * stats[1] * lng_ref[:, sl] + lnb_ref[:, sl]
        ws = jnp.where(causal, sw_ref[c], 0.0)
        parts = []
        for n in range(ts // CHUNK):
            parts.append(_bdot(ws, vn[n * CHUNK:(n + 1) * CHUNK, :]) + sb_ref[:, sl])
        mixed = parts[0] if len(parts) == 1 else jnp.concatenate(parts, axis=0)
        y_a = _gelu(au_ref[:, sl].astype(F32)) * mixed
        xbuf[LRU_HALO:LRU_HALO + ts, sl] = bx_ref[:, sl].astype(F32)
        xb = jnp.broadcast_to(lcb_ref[:, sl], (ts, LANES))
        for k in range(LRU_CONV_WIDTH):
            off = LRU_HALO - (LRU_CONV_WIDTH - 1) + k
            xb = xb + lcw_ref[k:k + 1, sl] * xbuf[off:off + ts, sl]
        r = _sigmoid(_bdot(xb, wa_ref[c]) + ba_ref[:, sl])
        ig = _sigmoid(_bdot(xb, wx_ref[c]) + bxb_ref[:, sl])
        lam = lam_ref[:, sl]
        softplus = jnp.maximum(-lam, 0.0) + jnp.log1p(jnp.exp(-jnp.abs(lam)))
        log_a = (-LRU_C) * r * softplus
        a = jnp.exp(log_a)
        bterm = jnp.sqrt(-jnp.tanh(log_a) * (a * a + 1.0)) * (ig * xb)
        h = _linear_scan(a, bterm, hstate[0:1, sl])
        hstate[0:1, sl] = h[ts - 1:ts, :]
        y_b = _gelu(bg_ref[:, sl].astype(F32)) * h
        z = (cvbuf[:, sl] - stats[2]) * stats[3] * cg_ref[:, sl] + cbt_ref[:, sl]
        y_c = z * _sigmoid(z)
        y = (_sigmoid(ga_ref[:, sl].astype(F32)) * y_a + _sigmoid(gb_ref[:, sl].astype(F32)) * y_b
             + _sigmoid(gc_ref[:, sl].astype(F32)) * y_c)
        y_ref[:, sl] = y.astype(y_ref.dtype)
        return carry

    lax.fori_loop(0, n_blk, pass2, 0, unroll=2)

    gbuf[0:CONV_HALO, :] = gbuf[ts:ts + CONV_HALO, :]
    xbuf[0:LRU_HALO, :] = xbuf[ts:ts + LRU_HALO, :]


def _hybrid_mixer(proj, sgu_ln_g, sgu_ln_b, sgu_w, sgu_b, lru_conv_w, lru_conv_b, lru_wa, lru_ba,
                  lru_wx, lru_bx, lru_lambda, conv_w, conv_b, conv_ln_g, conv_ln_b):
    bn, s, n9 = proj.shape
    width = n9 // 9
    n_grp = width // LANES
    assert sgu_w.shape == (n_grp, CHUNK, CHUNK) and lru_wa.shape == (n_grp, LANES, LANES)
    ts = CHUNK
    row = lambda v: v.reshape(1, width)
    sb = jnp.repeat(jnp.transpose(sgu_b), LANES, axis=1)
    cw = jnp.pad(conv_w, ((0, CONV_HALO - CONF_CONV_WIDTH), (0, 0)))
    lcw = jnp.pad(lru_conv_w, ((0, LRU_HALO - LRU_CONV_WIDTH), (0, 0)))

    def split(k):
        return pl.BlockSpec((None, ts, width), lambda bi, i, k=k: (bi, i, k))

    def full(shape):
        return pl.BlockSpec(shape, lambda bi, i: (0,) * len(shape))

    vec = full((1, width))
    grp = full((n_grp, LANES, LANES))
    return pl.pallas_call(
        _mixer_kernel,
        grid=(bn, s // ts),
        in_specs=[split(k) for k in range(9)] + [
            vec, vec, grp, full((CHUNK, width)), full((LRU_HALO, width)), vec, grp, vec, grp, vec,
            vec, full((CONV_HALO, width)), vec, vec, vec],
        out_specs=pl.BlockSpec((None, ts, width), lambda bi, i: (bi, i, 0)),
        out_shape=jax.ShapeDtypeStruct((bn, s, width), BF16),
        scratch_shapes=[
            pltpu.VMEM((ts, width), F32),
            pltpu.VMEM((ts, width), F32),
            pltpu.VMEM((CONV_HALO + ts, width), F32),
            pltpu.VMEM((LRU_HALO + ts, width), F32),
            pltpu.VMEM((4, ts, LANES), F32),
            pltpu.VMEM((8, width), F32),
            pltpu.VMEM((SUBLANES - 1, CONV_HALO + ts, LANES), F32),
        ],
        compiler_params=_cparams("arbitrary", "arbitrary"),
        name="hybrid_mixer",
    )(*([proj] * 9), row(sgu_ln_g), row(sgu_ln_b), sgu_w, sb, lcw, row(lru_conv_b), lru_wa,
      row(lru_ba), lru_wx, row(lru_bx), row(lru_lambda), cw, row(conv_b), row(conv_ln_g),
      row(conv_ln_b))


HI_MASK = 0xFFFF0000


def _pack_halves(v):
    n = v.shape[-1] // 2
    lo = lax.bitcast_convert_type(v[:, :n].astype(BF16).astype(F32), jnp.uint32)
    hi = lax.bitcast_convert_type(v[:, n:].astype(BF16).astype(F32), jnp.uint32)
    return (lo >> jnp.uint32(16)) | (hi & jnp.uint32(HI_MASK))


def _unpack_halves(p):
    lo = lax.bitcast_convert_type(p << jnp.uint32(16), F32)
    hi = lax.bitcast_convert_type(p & jnp.uint32(HI_MASK), F32)
    return lo, hi


def _outproj_kernel(y_ref, w_ref, b_ref, x_ref, mod_ref, g_ref, xo_ref, h_ref, hp_ref):
    mix = jnp.dot(y_ref[...], w_ref[...], preferred_element_type=F32) + b_ref[...]
    xn = x_ref[...] + mod_ref[2:3, :] * mix
    xo_ref[...] = xn
    ms = jnp.mean(xn * xn, axis=-1, keepdims=True)
    h = xn * lax.rsqrt(ms + NORM_EPS) * g_ref[...]
    h = h * (1.0 + mod_ref[4:5, :]) + mod_ref[3:4, :]
    h_ref[...] = h
    hp_ref[...] = _pack_halves(h)


def _out_projection(y, w_all, layer, b, x, mod, g):
    bn, s, d = x.shape
    width = y.shape[-1]
    tm = min(s, 256)
    blk = pl.BlockSpec((None, tm, d), lambda bi, i: (bi, i, 0))
    pblk = pl.BlockSpec((None, tm, d // 2), lambda bi, i: (bi, i, 0))
    return pl.pallas_call(
        _outproj_kernel,
        grid=(bn, s // tm),
        in_specs=[
            pl.BlockSpec((None, tm, width), lambda bi, i: (bi, i, 0)),
            pl.BlockSpec((None, width, d), lambda bi, i: (layer, 0, 0)),
            pl.BlockSpec((1, d), lambda bi, i: (0, 0)),
            blk,
            pl.BlockSpec((None, 6, d), lambda bi, i: (bi, 0, 0)),
            pl.BlockSpec((1, d), lambda bi, i: (0, 0)),
        ],
        out_specs=[blk, blk, pblk],
        out_shape=[jax.ShapeDtypeStruct((bn, s, d), F32), jax.ShapeDtypeStruct((bn, s, d), F32),
                   jax.ShapeDtypeStruct((bn, s, d // 2), jnp.uint32)],
        compiler_params=_cparams("arbitrary", "arbitrary"),
        name="out_projection",
    )(y, w_all, b.reshape(1, d), x, mod, g.reshape(1, d))


def _router_kernel(h_ref, rw_ref, rb_ref, idx_ref, w_ref, rank_ref, cnt_ref, carry_ref):
    n_exp = rw_ref.shape[0]
    tm = h_ref.shape[0]
    per_grp = n_exp // N_EXPERT_GROUPS
    neg = -jnp.inf

    @pl.when(pl.program_id(0) == 0)
    def _():
        carry_ref[...] = jnp.zeros(carry_ref.shape, F32)

    logits = lax.dot_general(rw_ref[...], h_ref[...], (((1,), (1,)), ((), ())),
                             precision=lax.Precision.HIGHEST, preferred_element_type=F32)
    scores = _sigmoid(logits)
    sel = (scores + rb_ref[...]).reshape(N_EXPERT_GROUPS, per_grp, tm)
    scores3 = scores.reshape(N_EXPERT_GROUPS, per_grp, tm)
    shape3 = (N_EXPERT_GROUPS, per_grp, tm)
    sub = lax.broadcasted_iota(jnp.int32, shape3, 1)
    gid = lax.broadcasted_iota(jnp.int32, shape3, 0)
    lin = gid * per_grp + sub

    m1 = jnp.max(sel, axis=1, keepdims=True)
    i1 = jnp.min(jnp.where(sel == m1, sub, per_grp), axis=1, keepdims=True)
    m2 = jnp.max(jnp.where(sub == i1, neg, sel), axis=1, keepdims=True)
    gscore = m1 + m2
    gidx = lax.broadcasted_iota(jnp.int32, gscore.shape, 0)
    gmask = jnp.zeros(gscore.shape, jnp.bool_)
    cur = gscore
    for _ in range(TOPK_GROUPS):
        m = jnp.max(cur, axis=0, keepdims=True)
        pick = gidx == jnp.min(jnp.where(cur == m, gidx, N_EXPERT_GROUPS), axis=0, keepdims=True)
        gmask = jnp.logical_or(gmask, pick)
        cur = jnp.where(pick, neg, cur)

    cur = jnp.where(gmask, sel, neg)
    chosen = jnp.zeros(shape3, F32)
    idx_rows, score_rows = [], []
    for _ in range(TOP_K):
        m = jnp.max(jnp.max(cur, axis=1, keepdims=True), axis=0, keepdims=True)
        first = jnp.min(jnp.min(jnp.where(cur == m, lin, n_exp), axis=1, keepdims=True),
                        axis=0, keepdims=True)
        pick = lin == first
        chosen = jnp.where(pick, 1.0, chosen)
        cur = jnp.where(pick, neg, cur)
        sc = jnp.sum(jnp.sum(jnp.where(pick, scores3, 0.0), axis=1, keepdims=True), axis=0,
                     keepdims=True)
        idx_rows.append(first.reshape(1, tm))
        score_rows.append(sc.reshape(1, tm))

    chosen2 = chosen.reshape(n_exp, tm)
    s_i = lax.broadcasted_iota(jnp.int32, (tm, tm), 0)
    t_i = lax.broadcasted_iota(jnp.int32, (tm, tm), 1)
    before = jnp.where(s_i < t_i, 1.0, 0.0).astype(BF16)
    prefix = jnp.dot(chosen2.astype(BF16), before, preferred_element_type=F32)
    rank3 = (prefix + carry_ref[...]).reshape(shape3)
    carry_ref[...] = carry_ref[...] + jnp.sum(chosen2, axis=1, keepdims=True)
    cnt_ref[...] = carry_ref[...]

    rank_rows = []
    for k in range(TOP_K):
        pick = lin == idx_rows[k].reshape(1, 1, tm)
        rk = jnp.sum(jnp.sum(jnp.where(pick, rank3, 0.0), axis=1, keepdims=True), axis=0,
                     keepdims=True)
        rank_rows.append(rk.reshape(1, tm))

    top_s = jnp.concatenate(score_rows, axis=0)
    idx_ref[...] = jnp.concatenate(idx_rows, axis=0)
    w_ref[...] = ROUTED_SCALE * top_s / jnp.sum(top_s, axis=0, keepdims=True)
    rank_ref[...] = jnp.concatenate(rank_rows, axis=0).astype(jnp.int32)


def _router(h2d, router_w, router_bias):
    t, d = h2d.shape
    n_exp = router_w.shape[1]
    tm = min(t, 256)
    row = pl.BlockSpec((TOP_K, tm), lambda i: (0, i))
    return pl.pallas_call(
        _router_kernel,
        grid=(t // tm,),
        in_specs=[
            pl.BlockSpec((tm, d), lambda i: (i, 0)),
            pl.BlockSpec((n_exp, d), lambda i: (0, 0)),
            pl.BlockSpec((n_exp, 1), lambda i: (0, 0)),
        ],
        out_specs=[row, row, row, pl.BlockSpec((n_exp, 1), lambda i: (0, 0))],
        out_shape=[
            jax.ShapeDtypeStruct((TOP_K, t), jnp.int32),
            jax.ShapeDtypeStruct((TOP_K, t), F32),
            jax.ShapeDtypeStruct((TOP_K, t), jnp.int32),
            jax.ShapeDtypeStruct((n_exp, 1), F32),
        ],
        scratch_shapes=[pltpu.VMEM((n_exp, 1), F32)],
        compiler_params=_cparams("arbitrary"),
        name="router",
    )(h2d, jnp.transpose(router_w), router_bias.reshape(n_exp, 1))


def _dispatch_kernel(tail_ref, dest_ref, h_ref, xs_ref, zbuf, sem):
    tm, row_words = h_ref.shape
    n_exp = tail_ref.shape[0]

    @pl.when(pl.program_id(0) == 0)
    def _():
        zbuf[...] = jnp.zeros(zbuf.shape, zbuf.dtype)

        def tail_copy(e):
            start = pl.multiple_of(tail_ref[e] * row_words, row_words)
            return pltpu.make_async_copy(zbuf, xs_ref.at[pl.ds(start, EXPERT_BLOCK * row_words)], sem)

        def zstart(e, carry):
            tail_copy(e).start()
            return carry

        def zwait(e, carry):
            tail_copy(e).wait()
            return carry

        lax.fori_loop(0, n_exp, zstart, 0)
        lax.fori_loop(0, n_exp, zwait, 0)

    def row_copy(t, k):
        start = pl.multiple_of(dest_ref[k, t], row_words)
        return pltpu.make_async_copy(h_ref.at[t], xs_ref.at[pl.ds(start, row_words)], sem)

    for t in range(tm):
        for k in range(TOP_K):
            row_copy(t, k).start()
    for t in range(tm):
        for k in range(TOP_K):
            row_copy(0, 0).wait()


def _dispatch(tail_start, dest_words, hp2d, n_rows):
    t, row_words = hp2d.shape
    tm = min(t, 256)
    return pl.pallas_call(
        _dispatch_kernel,
        grid_spec=pltpu.PrefetchScalarGridSpec(
            num_scalar_prefetch=1,
            grid=(t // tm,),
            in_specs=[
                pl.BlockSpec((TOP_K, tm), lambda i, tail: (0, i), memory_space=pltpu.SMEM),
                pl.BlockSpec((tm, row_words), lambda i, tail: (i, 0)),
            ],
            out_specs=pl.BlockSpec(memory_space=pl.ANY),
            scratch_shapes=[pltpu.VMEM((EXPERT_BLOCK * row_words,), hp2d.dtype),
                            pltpu.SemaphoreType.DMA(())],
        ),
        out_shape=jax.ShapeDtypeStruct((n_rows * row_words,), hp2d.dtype),
        compiler_params=_cparams("arbitrary"),
        name="dispatch",
    )(tail_start, dest_words, hp2d)


def _expert_kernel(be_ref, nu_ref, xs_hbm, wg_ref, wu_ref, wd_ref, y_hbm, xbuf, ybuf, wgb, wub, wdb,
                   in_sem, out_sem):
    i = pl.program_id(0)
    n_used = nu_ref[0]

    new_expert = jnp.logical_or(i == 0, be_ref[i] != be_ref[jnp.maximum(i - 1, 0)])

    @pl.when(jnp.logical_and(i < n_used, new_expert))
    def _():
        wgb[...] = wg_ref[...].astype(BF16)
        wub[...] = wu_ref[...].astype(BF16)
        wdb[...] = wd_ref[...].astype(BF16)

    n_slab = xbuf.shape[2] // LANES
    slot = lax.rem(i, 2)

    def slabs(hbm, buf, blk, slot_, sem, to_hbm):
        r0 = pl.multiple_of(blk * EXPERT_BLOCK, EXPERT_BLOCK)
        out = []
        for s in range(n_slab):
            far = hbm.at[pl.ds(r0, EXPERT_BLOCK), s, :]
            near = buf.at[slot_, :, pl.ds(s * LANES, LANES)]
            src, dst = (near, far) if to_hbm else (far, near)
            out.append(pltpu.make_async_copy(src, dst, sem.at[slot_]))
        return out

    def load(blk, slot_):
        return slabs(xs_hbm, xbuf, blk, slot_, in_sem, False)

    def store(blk, slot_):
        return slabs(y_hbm, ybuf, blk, slot_, out_sem, True)

    @pl.when(i == 0)
    def _():
        for c in load(0, 0):
            c.start()

    @pl.when(i + 1 < n_used)
    def _():
        for c in load(i + 1, 1 - slot):
            c.start()

    @pl.when(i < n_used)
    def _():
        for c in load(i, slot):
            c.wait()

        @pl.when(i >= 2)
        def _():
            for c in store(i - 2, slot):
                c.wait()

        half = xbuf.shape[2]
        x_lo, x_hi = _unpack_halves(xbuf[slot])
        x_lo = x_lo.astype(BF16)
        x_hi = x_hi.astype(BF16)
        g = (jnp.dot(x_lo, wgb[0:half, :], preferred_element_type=F32)
             + jnp.dot(x_hi, wgb[half:, :], preferred_element_type=F32))
        u = (jnp.dot(x_lo, wub[0:half, :], preferred_element_type=F32)
             + jnp.dot(x_hi, wub[half:, :], preferred_element_type=F32))
        hid = ((g * _sigmoid(g)) * u).astype(BF16)
        ybuf[slot] = _pack_halves(jnp.dot(hid, wdb[...], preferred_element_type=F32))
        for c in store(i, slot):
            c.start()

        @pl.when(i == n_used - 1)
        def _():
            for c in store(i, slot):
                c.wait()

            @pl.when(i >= 1)
            def _():
                for c in store(i - 1, 1 - slot):
                    c.wait()


def _expert_ffn(block_e, n_used, xs, wg_all, wu_all, wd_all, layer):
    n_rows, n_slab, _ = xs.shape
    _, n_exp, d, ff = wg_all.shape
    nb = n_rows // EXPERT_BLOCK
    row_words = n_slab * LANES

    def wsel(i, be, nu):
        return (layer, be[jnp.minimum(i, nu[0] - 1)], 0, 0)

    return pl.pallas_call(
        _expert_kernel,
        grid_spec=pltpu.PrefetchScalarGridSpec(
            num_scalar_prefetch=2,
            grid=(nb,),
            in_specs=[
                pl.BlockSpec(memory_space=pl.ANY),
                pl.BlockSpec((None, None, d, ff), wsel),
                pl.BlockSpec((None, None, d, ff), wsel),
                pl.BlockSpec((None, None, ff, d), wsel),
            ],
            out_specs=pl.BlockSpec(memory_space=pl.ANY),
            scratch_shapes=[
                pltpu.VMEM((2, EXPERT_BLOCK, row_words), jnp.uint32),
                pltpu.VMEM((2, EXPERT_BLOCK, row_words), jnp.uint32),
                pltpu.VMEM((d, ff), BF16),
                pltpu.VMEM((d, ff), BF16),
                pltpu.VMEM((ff, d), BF16),
                pltpu.SemaphoreType.DMA((2,)),
                pltpu.SemaphoreType.DMA((2,)),
            ],
        ),
        out_shape=jax.ShapeDtypeStruct((n_rows, n_slab, LANES), jnp.uint32),
        compiler_params=_cparams("arbitrary"),
        name="expert_ffn",
    )(block_e, n_used, xs, wg_all, wu_all, wd_all)


COMBINE_TILE = 128


def _combine_kernel(d_first_ref, d_b_ref, d_next_ref, y_hbm, h_ref, tw_ref, x_ref, mod_ref, sg_ref,
                    su_ref, sd_ref, fg_ref, o_ref, ybuf_a, ybuf_b, sem_a, sem_b, *, final_norm):
    j = pl.program_id(0) * pl.num_programs(1) + pl.program_id(1)
    last = pl.num_programs(0) * pl.num_programs(1) - 1
    tile, row_words = ybuf_a.shape[1], ybuf_a.shape[2]

    def row_copy(d_ref, buf, sem, t, k):
        start = pl.multiple_of(d_ref[k, t], row_words)
        return pltpu.make_async_copy(y_hbm.at[pl.ds(start, row_words)], buf.at[k, t], sem)

    def start_rows(d_ref, buf, sem, lo, hi):
        for t in range(lo, hi):
            for k in range(TOP_K):
                row_copy(d_ref, buf, sem, t, k).start()

    def wait_all(d_ref, buf, sem):
        for t in range(tile):
            for k in range(TOP_K):
                row_copy(d_ref, buf, sem, 0, 0).wait()

    def shared_ffn(rows):
        hb = h_ref[rows, :].astype(BF16)
        g = jnp.dot(hb, sg_ref[...], preferred_element_type=F32)
        u = jnp.dot(hb, su_ref[...], preferred_element_type=F32)
        return jnp.dot(((g * _sigmoid(g)) * u).astype(BF16), sd_ref[...], preferred_element_type=F32)

    def finish(rows, acc, buf):
        acc_lo = acc[:, :row_words]
        acc_hi = acc[:, row_words:]
        for k in range(TOP_K):
            y_lo, y_hi = _unpack_halves(buf[k])
            wk = tw_ref[rows, k:k + 1]
            acc_lo = acc_lo + wk * y_lo
            acc_hi = acc_hi + wk * y_hi
        xo = x_ref[rows, :] + mod_ref[5:6, :] * jnp.concatenate([acc_lo, acc_hi], axis=1)
        if final_norm:
            ms = jnp.mean(xo * xo, axis=-1, keepdims=True)
            xo = xo * lax.rsqrt(ms + NORM_EPS) * fg_ref[...]
        o_ref[rows, :] = xo

    rows_a = slice(0, tile)
    rows_b = slice(tile, 2 * tile)

    @pl.when(j == 0)
    def _():
        def first(t, carry):
            for k in range(TOP_K):
                row_copy(d_first_ref, ybuf_a, sem_a, t, k).start()
            return carry
        lax.fori_loop(0, tile, first, 0)

    start_rows(d_b_ref, ybuf_b, sem_b, 0, tile // 2)
    acc_a = shared_ffn(rows_a)
    wait_all(d_b_ref, ybuf_a, sem_a)
    start_rows(d_b_ref, ybuf_b, sem_b, tile // 2, tile)
    finish(rows_a, acc_a, ybuf_a)
    acc_b = shared_ffn(rows_b)
    wait_all(d_b_ref, ybuf_b, sem_b)
    start_rows(d_next_ref, ybuf_a, sem_a, 0, tile)
    finish(rows_b, acc_b, ybuf_b)

    @pl.when(j == last)
    def _():
        wait_all(d_next_ref, ybuf_a, sem_a)


def _combine(dest, y, h, tw, x, mod, sg_all, su_all, sd_all, layer, fg, final_norm):
    bn, s, d = x.shape
    ff = sg_all.shape[2]
    tile = COMBINE_TILE
    tm = 2 * tile
    assert s % tm == 0
    per_b = s // tm
    n_tiles = bn * s // tile
    blk = pl.BlockSpec((None, tm, d), lambda bi, i: (bi, i, 0))

    def smem_tile(index):
        return pl.BlockSpec((TOP_K, tile), lambda bi, i: (0, index(bi * per_b + i)),
                            memory_space=pltpu.SMEM)

    return pl.pallas_call(
        functools.partial(_combine_kernel, final_norm=final_norm),
        grid=(bn, per_b),
        in_specs=[
            smem_tile(lambda j: 0),
            smem_tile(lambda j: 2 * j + 1),
            smem_tile(lambda j: jnp.minimum(2 * j + 2, n_tiles - 1)),
            pl.BlockSpec(memory_space=pl.ANY),
            blk,
            pl.BlockSpec((None, tm, TOP_K), lambda bi, i: (bi, i, 0)),
            blk,
            pl.BlockSpec((None, 6, d), lambda bi, i: (bi, 0, 0)),
            pl.BlockSpec((None, d, ff), lambda bi, i: (layer, 0, 0)),
            pl.BlockSpec((None, d, ff), lambda bi, i: (layer, 0, 0)),
            pl.BlockSpec((None, ff, d), lambda bi, i: (layer, 0, 0)),
            pl.BlockSpec((1, d), lambda bi, i: (0, 0)),
        ],
        out_specs=blk,
        out_shape=jax.ShapeDtypeStruct((bn, s, d), F32),
        scratch_shapes=[pltpu.VMEM((TOP_K, tile, d // 2), jnp.uint32),
                        pltpu.VMEM((TOP_K, tile, d // 2), jnp.uint32),
                        pltpu.SemaphoreType.DMA(()), pltpu.SemaphoreType.DMA(())],
        compiler_params=_cparams("arbitrary", "arbitrary"),
        name="combine",
    )(dest, dest, dest, y, h, tw, x, mod, sg_all, su_all, sd_all, fg.reshape(1, d))


def _moe_ffn(h, hp, x, mod, router_w, router_bias, exp_gate, exp_up, exp_down, sh_gate, sh_up, sh_down,
             layer, final_g, final_norm):
    bn, s, d = h.shape
    t = bn * s
    n_exp = router_w.shape[1]
    h2d = h.reshape(t, d)
    top_idx, top_w, rank, counts = _router(h2d, router_w, router_bias)
    cnt = counts.reshape(n_exp).astype(jnp.int32)
    padded = (cnt + EXPERT_BLOCK - 1) // EXPERT_BLOCK * EXPERT_BLOCK
    pends = jnp.cumsum(padded)
    pstarts = pends - padded
    experts = jnp.arange(n_exp, dtype=jnp.int32).reshape(n_exp, 1, 1)
    dest = jnp.sum(jnp.where(top_idx[None] == experts, pstarts.reshape(n_exp, 1, 1), 0), axis=0) + rank
    row_words = d // 2
    dest_words = dest * row_words
    nb = (t * TOP_K) // EXPERT_BLOCK + n_exp
    block_start = jnp.arange(nb, dtype=jnp.int32) * EXPERT_BLOCK
    block_e = jnp.minimum(jnp.sum(pends[None, :] <= block_start[:, None], axis=1), n_exp - 1)
    n_used = (pends[-1] // EXPERT_BLOCK).reshape(1).astype(jnp.int32)
    tail_start = jnp.maximum(pends - EXPERT_BLOCK, 0).astype(jnp.int32)
    n_rows = nb * EXPERT_BLOCK
    n_slab = row_words // LANES
    xs = _dispatch(tail_start, dest_words, hp.reshape(t, row_words), n_rows)
    y = _expert_ffn(block_e.astype(jnp.int32), n_used, xs.reshape(n_rows, n_slab, LANES),
                    exp_gate, exp_up, exp_down, layer)
    tw = jnp.transpose(top_w).reshape(bn, s, TOP_K)
    return _combine(dest_words, y.reshape(n_rows * row_words), h, tw, x, mod, sh_gate, sh_up,
                    sh_down, layer, final_g, final_norm)


def kernel(x, c, ada_w, ada_b, norm_mix_g, w_in, b_in, sgu_ln_g, sgu_ln_b, sgu_w, sgu_b, lru_conv_w, lru_conv_b, lru_wa, lru_ba, lru_wx, lru_bx, lru_lambda, conv_w, conv_b, conv_ln_g, conv_ln_b, w_out, b_out, norm_ffn_g, router_w, router_bias, exp_gate, exp_up, exp_down, sh_gate, sh_up, sh_down, final_norm_g):
    depth = ada_w.shape[0]
    bn, s, d = x.shape
    mod_all = _ada_modulation(c, ada_w, ada_b).reshape(depth, bn, 6, d)
    w_in_b, w_out_b = w_in.astype(BF16), w_out.astype(BF16)
    exp_b = [exp_gate, exp_up, exp_down]
    sh_b = [w.astype(BF16) for w in (sh_gate, sh_up, sh_down)]
    for l in range(depth):
        mod = mod_all[l]
        proj = _in_projection(x, mod, norm_mix_g[l], w_in_b, l, b_in[l])
        y = _hybrid_mixer(proj, sgu_ln_g[l], sgu_ln_b[l], sgu_w[l], sgu_b[l], lru_conv_w[l],
                          lru_conv_b[l], lru_wa[l], lru_ba[l], lru_wx[l], lru_bx[l], lru_lambda[l],
                          conv_w[l], conv_b[l], conv_ln_g[l], conv_ln_b[l])
        x, h, hp = _out_projection(y, w_out_b, l, b_out[l], x, mod, norm_ffn_g[l])
        x = _moe_ffn(h, hp, x, mod, router_w[l], router_bias[l], *exp_b, *sh_b, l, final_norm_g,
                     l == depth - 1)
    return x
```

```python
import functools

import jax
import jax.numpy as jnp
from jax import lax
from jax.experimental import pallas as pl
from jax.experimental.pallas import tpu as pltpu

F32 = jnp.float32
BF16 = jnp.bfloat16

LANES = 128
SUBLANES = 8
NORM_EPS = 1e-6
CHUNK = 128
LRU_C = 8.0
LRU_CONV_WIDTH = 4
CONF_CONV_WIDTH = 31
CONV_HALO = 32
LRU_HALO = 8
N_EXPERT_GROUPS = 8
TOPK_GROUPS = 4
TOP_K = 8
ROUTED_SCALE = 2.5
EXPERT_BLOCK = 512
VMEM_LIMIT = 56 * 1024 * 1024


def _cparams(*sem):
    return pltpu.CompilerParams(dimension_semantics=sem, vmem_limit_bytes=VMEM_LIMIT)


def _gelu(x):
    return 0.5 * x * (1.0 + jnp.tanh(0.7978845608028654 * (x + 0.044715 * (x * x * x))))


def _sigmoid(x):
    return 1.0 / (1.0 + jnp.exp(-x))


def _bdot(a, b):
    return jnp.dot(a.astype(BF16), b.astype(BF16), preferred_element_type=F32)


def _ada_kernel(c_ref, w_ref, b_ref, o_ref):
    c = c_ref[...]
    cond = c * _sigmoid(c)
    o_ref[...] = _bdot(cond, w_ref[...]) + b_ref[...]


def _ada_modulation(c, ada_w, ada_b):
    n_layers, d, n_out = ada_w.shape
    bn = c.shape[0]
    tn = 1024
    return pl.pallas_call(
        _ada_kernel,
        grid=(n_layers, n_out // tn),
        in_specs=[
            pl.BlockSpec((bn, d), lambda l, j: (0, 0)),
            pl.BlockSpec((None, d, tn), lambda l, j: (l, 0, j)),
            pl.BlockSpec((None, 1, tn), lambda l, j: (l, 0, j)),
        ],
        out_specs=pl.BlockSpec((None, bn, tn), lambda l, j: (l, 0, j)),
        out_shape=jax.ShapeDtypeStruct((n_layers, bn, n_out), F32),
        compiler_params=_cparams("arbitrary", "arbitrary"),
        name="ada_modulation",
    )(c, ada_w, ada_b.reshape(n_layers, 1, n_out))


def _inproj_kernel(x_ref, mod_ref, g_ref, w_ref, b_ref, o_ref, h_ref):
    @pl.when(pl.program_id(2) == 0)
    def _():
        x = x_ref[...]
        ms = jnp.mean(x * x, axis=-1, keepdims=True)
        y = x * lax.rsqrt(ms + NORM_EPS) * g_ref[...]
        h_ref[...] = (y * (1.0 + mod_ref[1:2, :]) + mod_ref[0:1, :]).astype(BF16)

    acc = jnp.dot(h_ref[...], w_ref[...], preferred_element_type=F32)
    o_ref[...] = (acc + b_ref[...]).astype(o_ref.dtype)


def _in_projection(x, mod, g, w_all, layer, b):
    bn, s, d = x.shape
    n_out = w_all.shape[2]
    tm = min(s, 1024)
    tn = 2048
    return pl.pallas_call(
        _inproj_kernel,
        grid=(bn, s // tm, n_out // tn),
        in_specs=[
            pl.BlockSpec((None, tm, d), lambda bi, i, j: (bi, i, 0)),
            pl.BlockSpec((None, 6, d), lambda bi, i, j: (bi, 0, 0)),
            pl.BlockSpec((1, d), lambda bi, i, j: (0, 0)),
            pl.BlockSpec((None, d, tn), lambda bi, i, j: (layer, 0, j)),
            pl.BlockSpec((1, tn), lambda bi, i, j: (0, j)),
        ],
        out_specs=pl.BlockSpec((None, tm, tn), lambda bi, i, j: (bi, i, j)),
        out_shape=jax.ShapeDtypeStruct((bn, s, n_out), BF16),
        scratch_shapes=[pltpu.VMEM((tm, d), BF16)],
        compiler_params=_cparams("arbitrary", "arbitrary", "arbitrary"),
        name="in_projection",
    )(x, mod, g.reshape(1, d), w_all, b.reshape(1, n_out))


def _linear_scan(a, b, h0):
    n = a.shape[0]
    rows = lax.broadcasted_iota(jnp.int32, a.shape, 0)
    d = 1
    while d < n:
        keep = rows >= d
        b = jnp.where(keep, b + a * pltpu.roll(b, d, 0), b)
        a = jnp.where(keep, a * pltpu.roll(a, d, 0), a)
        d *= 2
    return b + a * h0


def _mixer_kernel(au_ref, av_ref, bx_ref, bg_ref, ca_ref, cb_ref, ga_ref, gb_ref, gc_ref,
                  lng_ref, lnb_ref, sw_ref, sb_ref, lcw_ref, lcb_ref, wa_ref, ba_ref, wx_ref,
                  bxb_ref, lam_ref, cw_ref, cvb_ref, cg_ref, cbt_ref,
                  y_ref, vbuf, cvbuf, gbuf, xbuf, stats, hstate, shbuf):
    ts, width = vbuf.shape
    n_blk = width // LANES

    @pl.when(pl.program_id(1) == 0)
    def _():
        gbuf[0:CONV_HALO, :] = jnp.zeros((CONV_HALO, width), F32)
        xbuf[0:LRU_HALO, :] = jnp.zeros((LRU_HALO, width), F32)
        hstate[...] = jnp.zeros(hstate.shape, F32)

    stats[...] = jnp.zeros(stats.shape, F32)

    def lanes(c):
        return pl.ds(pl.multiple_of(c * LANES, LANES), LANES)

    def pass1(c, carry):
        sl = lanes(c)
        v = _gelu(av_ref[:, sl].astype(F32))
        vbuf[:, sl] = v
        stats[0] += v
        stats[1] += v * v
        glu = ca_ref[:, sl].astype(F32) * _sigmoid(cb_ref[:, sl].astype(F32))
        gbuf[CONV_HALO:CONV_HALO + ts, sl] = glu
        for b in range(1, SUBLANES):
            shbuf[b - 1, SUBLANES:, :] = gbuf[SUBLANES - b:CONV_HALO + ts - b, sl]
        acc = jnp.broadcast_to(cvb_ref[:, sl], (ts, LANES))
        for lag in range(CONF_CONV_WIDTH):
            a, b = divmod(lag, SUBLANES)
            off = CONV_HALO - SUBLANES * a
            tap = gbuf[off:off + ts, sl] if b == 0 else shbuf[b - 1, off:off + ts, :]
            k = CONF_CONV_WIDTH - 1 - lag
            acc = acc + cw_ref[k:k + 1, sl] * tap
        cvbuf[:, sl] = acc
        stats[2] += acc
        stats[3] += acc * acc
        return carry

    lax.fori_loop(0, n_blk, pass1, 0)

    inv_w = 1.0 / width
    for q in (0, 2):
        mean = jnp.sum(stats[q], axis=-1, keepdims=True) * inv_w
        ex2 = jnp.sum(stats[q + 1], axis=-1, keepdims=True) * inv_w
        rstd = lax.rsqrt(ex2 - mean * mean + NORM_EPS)
        stats[q] = jnp.broadcast_to(mean, (ts, LANES))
        stats[q + 1] = jnp.broadcast_to(rstd, (ts, LANES))

    r_i = lax.broadcasted_iota(jnp.int32, (CHUNK, CHUNK), 0)
    c_i = lax.broadcasted_iota(jnp.int32, (CHUNK, CHUNK), 1)
    causal = r_i >= c_i

    def pass2(c, carry):
        sl = lanes(c)
        vn = (vbuf[:, sl] - stats[0]) * stats[1] * lng_ref[:, sl] + lnb_ref[:, sl]
        ws = jnp.where(causal, sw_ref[c], 0.0)
        parts = []
        for n in range(ts // CHUNK):
            parts.append(_bdot(ws, vn[n * CHUNK:(n + 1) * CHUNK, :]) + sb_ref[:, sl])
        mixed = parts[0] if len(parts) == 1 else jnp.concatenate(parts, axis=0)
        y_a = _gelu(au_ref[:, sl].astype(F32)) * mixed
        xbuf[LRU_HALO:LRU_HALO + ts, sl] = bx_ref[:, sl].astype(F32)
        xb = jnp.broadcast_to(lcb_ref[:, sl], (ts, LANES))
        for k in range(LRU_CONV_WIDTH):
            off = LRU_HALO - (LRU_CONV_WIDTH - 1) + k
            xb = xb + lcw_ref[k:k + 1, sl] * xbuf[off:off + ts, sl]
        r = _sigmoid(_bdot(xb, wa_ref[c]) + ba_ref[:, sl])
        ig = _sigmoid(_bdot(xb, wx_ref[c]) + bxb_ref[:, sl])
        lam = lam_ref[:, sl]
        softplus = jnp.maximum(-lam, 0.0) + jnp.log1p(jnp.exp(-jnp.abs(lam)))
        log_a = (-LRU_C) * r * softplus
        a = jnp.exp(log_a)
        bterm = jnp.sqrt(-jnp.tanh(log_a) * (a * a + 1.0)) * (ig * xb)
        h = _linear_scan(a, bterm, hstate[0:1, sl])
        hstate[0:1, sl] = h[ts - 1:ts, :]
        y_b = _gelu(bg_ref[:, sl].astype(F32)) * h
        z = (cvbuf[:, sl] - stats[2]) * stats[3] * cg_ref[:, sl] + cbt_ref[:, sl]
        y_c = z * _sigmoid(z)
        y = (_sigmoid(ga_ref[:, sl].astype(F32)) * y_a + _sigmoid(gb_ref[:, sl].astype(F32)) * y_b
             + _sigmoid(gc_ref[:, sl].astype(F32)) * y_c)
        y_ref[:, sl] = y.astype(y_ref.dtype)
        return carry

    lax.fori_loop(0, n_blk, pass2, 0, unroll=2)

    gbuf[0:CONV_HALO, :] = gbuf[ts:ts + CONV_HALO, :]
    xbuf[0:LRU_HALO, :] = xbuf[ts:ts + LRU_HALO, :]


def _hybrid_mixer(proj, sgu_ln_g, sgu_ln_b, sgu_w, sgu_b, lru_conv_w, lru_conv_b, lru_wa, lru_ba,
                  lru_wx, lru_bx, lru_lambda, conv_w, conv_b, conv_ln_g, conv_ln_b):
    bn, s, n9 = proj.shape
    width = n9 // 9
    n_grp = width // LANES
    assert sgu_w.shape == (n_grp, CHUNK, CHUNK) and lru_wa.shape == (n_grp, LANES, LANES)
    ts = CHUNK
    row = lambda v: v.reshape(1, width)
    sb = jnp.repeat(jnp.transpose(sgu_b), LANES, axis=1)
    cw = jnp.pad(conv_w, ((0, CONV_HALO - CONF_CONV_WIDTH), (0, 0)))
    lcw = jnp.pad(lru_conv_w, ((0, LRU_HALO - LRU_CONV_WIDTH), (0, 0)))

    def split(k):
        return pl.BlockSpec((None, ts, width), lambda bi, i, k=k: (bi, i, k))

    def full(shape):
        return pl.BlockSpec(shape, lambda bi, i: (0,) * len(shape))

    vec = full((1, width))
    grp = full((n_grp, LANES, LANES))
    return pl.pallas_call(
        _mixer_kernel,
        grid=(bn, s // ts),
        in_specs=[split(k) for k in range(9)] + [
            vec, vec, grp, full((CHUNK, width)), full((LRU_HALO, width)), vec, grp, vec, grp, vec,
            vec, full((CONV_HALO, width)), vec, vec, vec],
        out_specs=pl.BlockSpec((None, ts, width), lambda bi, i: (bi, i, 0)),
        out_shape=jax.ShapeDtypeStruct((bn, s, width), BF16),
        scratch_shapes=[
            pltpu.VMEM((ts, width), F32),
            pltpu.VMEM((ts, width), F32),
            pltpu.VMEM((CONV_HALO + ts, width), F32),
            pltpu.VMEM((LRU_HALO + ts, width), F32),
            pltpu.VMEM((4, ts, LANES), F32),
            pltpu.VMEM((8, width), F32),
            pltpu.VMEM((SUBLANES - 1, CONV_HALO + ts, LANES), F32),
        ],
        compiler_params=_cparams("arbitrary", "arbitrary"),
        name="hybrid_mixer",
    )(*([proj] * 9), row(sgu_ln_g), row(sgu_ln_b), sgu_w, sb, lcw, row(lru_conv_b), lru_wa,
      row(lru_ba), lru_wx, row(lru_bx), row(lru_lambda), cw, row(conv_b), row(conv_ln_g),
      row(conv_ln_b))


HI_MASK = 0xFFFF0000


def _pack_halves(v):
    n = v.shape[-1] // 2
    lo = lax.bitcast_convert_type(v[:, :n].astype(BF16).astype(F32), jnp.uint32)
    hi = lax.bitcast_convert_type(v[:, n:].astype(BF16).astype(F32), jnp.uint32)
    return (lo >> jnp.uint32(16)) | (hi & jnp.uint32(HI_MASK))


def _unpack_halves(p):
    lo = lax.bitcast_convert_type(p << jnp.uint32(16), F32)
    hi = lax.bitcast_convert_type(p & jnp.uint32(HI_MASK), F32)
    return lo, hi


def _outproj_kernel(y_ref, w_ref, b_ref, x_ref, mod_ref, g_ref, xo_ref, h_ref, hp_ref):
    mix = jnp.dot(y_ref[...], w_ref[...], preferred_element_type=F32) + b_ref[...]
    xn = x_ref[...] + mod_ref[2:3, :] * mix
    xo_ref[...] = xn
    ms = jnp.mean(xn * xn, axis=-1, keepdims=True)
    h = xn * lax.rsqrt(ms + NORM_EPS) * g_ref[...]
    h = h * (1.0 + mod_ref[4:5, :]) + mod_ref[3:4, :]
    h_ref[...] = h
    hp_ref[...] = _pack_halves(h)


def _out_projection(y, w_all, layer, b, x, mod, g):
    bn, s, d = x.shape
    width = y.shape[-1]
    tm = min(s, 256)
    blk = pl.BlockSpec((None, tm, d), lambda bi, i: (bi, i, 0))
    pblk = pl.BlockSpec((None, tm, d // 2), lambda bi, i: (bi, i, 0))
    return pl.pallas_call(
        _outproj_kernel,
        grid=(bn, s // tm),
        in_specs=[
            pl.BlockSpec((None, tm, width), lambda bi, i: (bi, i, 0)),
            pl.BlockSpec((None, width, d), lambda bi, i: (layer, 0, 0)),
            pl.BlockSpec((1, d), lambda bi, i: (0, 0)),
            blk,
            pl.BlockSpec((None, 6, d), lambda bi, i: (bi, 0, 0)),
            pl.BlockSpec((1, d), lambda bi, i: (0, 0)),
        ],
        out_specs=[blk, blk, pblk],
        out_shape=[jax.ShapeDtypeStruct((bn, s, d), F32), jax.ShapeDtypeStruct((bn, s, d), F32),
                   jax.ShapeDtypeStruct((bn, s, d // 2), jnp.uint32)],
        compiler_params=_cparams("arbitrary", "arbitrary"),
        name="out_projection",
    )(y, w_all, b.reshape(1, d), x, mod, g.reshape(1, d))


def _router_kernel(h_ref, rw_ref, rb_ref, idx_ref, w_ref, rank_ref, cnt_ref, carry_ref):
    n_exp = rw_ref.shape[0]
    tm = h_ref.shape[0]
    per_grp = n_exp // N_EXPERT_GROUPS
    neg = -jnp.inf

    @pl.when(pl.program_id(0) == 0)
    def _():
        carry_ref[...] = jnp.zeros(carry_ref.shape, F32)

    logits = lax.dot_general(rw_ref[...], h_ref[...], (((1,), (1,)), ((), ())),
                             precision=lax.Precision.HIGHEST, preferred_element_type=F32)
    scores = _sigmoid(logits)
    sel = (scores + rb_ref[...]).reshape(N_EXPERT_GROUPS, per_grp, tm)
    scores3 = scores.reshape(N_EXPERT_GROUPS, per_grp, tm)
    shape3 = (N_EXPERT_GROUPS, per_grp, tm)
    sub = lax.broadcasted_iota(jnp.int32, shape3, 1)
    gid = lax.broadcasted_iota(jnp.int32, shape3, 0)
    lin = gid * per_grp + sub

    m1 = jnp.max(sel, axis=1, keepdims=True)
    i1 = jnp.min(jnp.where(sel == m1, sub, per_grp), axis=1, keepdims=True)
    m2 = jnp.max(jnp.where(sub == i1, neg, sel), axis=1, keepdims=True)
    gscore = m1 + m2
    gidx = lax.broadcasted_iota(jnp.int32, gscore.shape, 0)
    gmask = jnp.zeros(gscore.shape, jnp.bool_)
    cur = gscore
    for _ in range(TOPK_GROUPS):
        m = jnp.max(cur, axis=0, keepdims=True)
        pick = gidx == jnp.min(jnp.where(cur == m, gidx, N_EXPERT_GROUPS), axis=0, keepdims=True)
        gmask = jnp.logical_or(gmask, pick)
        cur = jnp.where(pick, neg, cur)

    cur = jnp.where(gmask, sel, neg)
    chosen = jnp.zeros(shape3, F32)
    idx_rows, score_rows = [], []
    for _ in range(TOP_K):
        m = jnp.max(jnp.max(cur, axis=1, keepdims=True), axis=0, keepdims=True)
        first = jnp.min(jnp.min(jnp.where(cur == m, lin, n_exp), axis=1, keepdims=True),
                        axis=0, keepdims=True)
        pick = lin == first
        chosen = jnp.where(pick, 1.0, chosen)
        cur = jnp.where(pick, neg, cur)
        sc = jnp.sum(jnp.sum(jnp.where(pick, scores3, 0.0), axis=1, keepdims=True), axis=0,
                     keepdims=True)
        idx_rows.append(first.reshape(1, tm))
        score_rows.append(sc.reshape(1, tm))

    chosen2 = chosen.reshape(n_exp, tm)
    s_i = lax.broadcasted_iota(jnp.int32, (tm, tm), 0)
    t_i = lax.broadcasted_iota(jnp.int32, (tm, tm), 1)
    before = jnp.where(s_i < t_i, 1.0, 0.0).astype(BF16)
    prefix = jnp.dot(chosen2.astype(BF16), before, preferred_element_type=F32)
    rank3 = (prefix + carry_ref[...]).reshape(shape3)
    carry_ref[...] = carry_ref[...] + jnp.sum(chosen2, axis=1, keepdims=True)
    cnt_ref[...] = carry_ref[...]

    rank_rows = []
    for k in range(TOP_K):
        pick = lin == idx_rows[k].reshape(1, 1, tm)
        rk = jnp.sum(jnp.sum(jnp.where(pick, rank3, 0.0), axis=1, keepdims=True), axis=0,
                     keepdims=True)
        rank_rows.append(rk.reshape(1, tm))

    top_s = jnp.concatenate(score_rows, axis=0)
    idx_ref[...] = jnp.concatenate(idx_rows, axis=0)
    w_ref[...] = ROUTED_SCALE * top_s / jnp.sum(top_s, axis=0, keepdims=True)
    rank_ref[...] = jnp.concatenate(rank_rows, axis=0).astype(jnp.int32)


def _router(h2d, router_w, router_bias):
    t, d = h2d.shape
    n_exp = router_w.shape[1]
    tm = min(t, 256)
    row = pl.BlockSpec((TOP_K, tm), lambda i: (0, i))
    return pl.pallas_call(
        _router_kernel,
        grid=(t // tm,),
        in_specs=[
            pl.BlockSpec((tm, d), lambda i: (i, 0)),
            pl.BlockSpec((n_exp, d), lambda i: (0, 0)),
            pl.BlockSpec((n_exp, 1), lambda i: (0, 0)),
        ],
        out_specs=[row, row, row, pl.BlockSpec((n_exp, 1), lambda i: (0, 0))],
        out_shape=[
            jax.ShapeDtypeStruct((TOP_K, t), jnp.int32),
            jax.ShapeDtypeStruct((TOP_K, t), F32),
            jax.ShapeDtypeStruct((TOP_K, t), jnp.int32),
            jax.ShapeDtypeStruct((n_exp, 1), F32),
        ],
        scratch_shapes=[pltpu.VMEM((n_exp, 1), F32)],
        compiler_params=_cparams("arbitrary"),
        name="router",
    )(h2d, jnp.transpose(router_w), router_bias.reshape(n_exp, 1))


def _dispatch_kernel(tail_ref, dest_ref, h_ref, xs_ref, zbuf, sem):
    tm, row_words = h_ref.shape
    n_exp = tail_ref.shape[0]

    @pl.when(pl.program_id(0) == 0)
    def _():
        zbuf[...] = jnp.zeros(zbuf.shape, zbuf.dtype)

        def tail_copy(e):
            start = pl.multiple_of(tail_ref[e] * row_words, row_words)
            return pltpu.make_async_copy(zbuf, xs_ref.at[pl.ds(start, EXPERT_BLOCK * row_words)], sem)

        def zstart(e, carry):
            tail_copy(e).start()
            return carry

        def zwait(e, carry):
            tail_copy(e).wait()
            return carry

        lax.fori_loop(0, n_exp, zstart, 0)
        lax.fori_loop(0, n_exp, zwait, 0)

    def row_copy(t, k):
        start = pl.multiple_of(dest_ref[k, t], row_words)
        return pltpu.make_async_copy(h_ref.at[t], xs_ref.at[pl.ds(start, row_words)], sem)

    for t in range(tm):
        for k in range(TOP_K):
            row_copy(t, k).start(priority=k % 2)
    for t in range(tm):
        for k in range(TOP_K):
            row_copy(0, 0).wait()


def _dispatch(tail_start, dest_words, hp2d, n_rows):
    t, row_words = hp2d.shape
    tm = min(t, 256)
    return pl.pallas_call(
        _dispatch_kernel,
        grid_spec=pltpu.PrefetchScalarGridSpec(
            num_scalar_prefetch=1,
            grid=(t // tm,),
            in_specs=[
                pl.BlockSpec((TOP_K, tm), lambda i, tail: (0, i), memory_space=pltpu.SMEM),
                pl.BlockSpec((tm, row_words), lambda i, tail: (i, 0)),
            ],
            out_specs=pl.BlockSpec(memory_space=pl.ANY),
            scratch_shapes=[pltpu.VMEM((EXPERT_BLOCK * row_words,), hp2d.dtype),
                            pltpu.SemaphoreType.DMA(())],
        ),
        out_shape=jax.ShapeDtypeStruct((n_rows * row_words,), hp2d.dtype),
        compiler_params=_cparams("arbitrary"),
        name="dispatch",
    )(tail_start, dest_words, hp2d)


def _expert_kernel(be_ref, nu_ref, xs_hbm, wg_ref, wu_ref, wd_ref, y_hbm, xbuf, ybuf, wgb, wub, wdb,
                   in_sem, out_sem):
    i = pl.program_id(0)
    n_used = nu_ref[0]

    new_expert = jnp.logical_or(i == 0, be_ref[i] != be_ref[jnp.maximum(i - 1, 0)])

    @pl.when(jnp.logical_and(i < n_used, new_expert))
    def _():
        wgb[...] = wg_ref[...].astype(BF16)
        wub[...] = wu_ref[...].astype(BF16)
        wdb[...] = wd_ref[...].astype(BF16)

    n_slab = xbuf.shape[2] // LANES
    slot = lax.rem(i, 2)

    def slabs(hbm, buf, blk, slot_, sem, to_hbm):
        r0 = pl.multiple_of(blk * EXPERT_BLOCK, EXPERT_BLOCK)
        out = []
        for s in range(n_slab):
            far = hbm.at[pl.ds(r0, EXPERT_BLOCK), s, :]
            near = buf.at[slot_, :, pl.ds(s * LANES, LANES)]
            src, dst = (near, far) if to_hbm else (far, near)
            out.append(pltpu.make_async_copy(src, dst, sem.at[slot_]))
        return out

    def load(blk, slot_):
        return slabs(xs_hbm, xbuf, blk, slot_, in_sem, False)

    def store(blk, slot_):
        return slabs(y_hbm, ybuf, blk, slot_, out_sem, True)

    @pl.when(i == 0)
    def _():
        for c in load(0, 0):
            c.start()

    @pl.when(i + 1 < n_used)
    def _():
        for c in load(i + 1, 1 - slot):
            c.start()

    @pl.when(i < n_used)
    def _():
        for c in load(i, slot):
            c.wait()

        @pl.when(i >= 2)
        def _():
            for c in store(i - 2, slot):
                c.wait()

        half = xbuf.shape[2]
        x_lo, x_hi = _unpack_halves(xbuf[slot])
        x_lo = x_lo.astype(BF16)
        x_hi = x_hi.astype(BF16)
        g = (jnp.dot(x_lo, wgb[0:half, :], preferred_element_type=F32)
             + jnp.dot(x_hi, wgb[half:, :], preferred_element_type=F32))
        u = (jnp.dot(x_lo, wub[0:half, :], preferred_element_type=F32)
             + jnp.dot(x_hi, wub[half:, :], preferred_element_type=F32))
        hid = ((g * _sigmoid(g)) * u).astype(BF16)
        ybuf[slot] = _pack_halves(jnp.dot(hid, wdb[...], preferred_element_type=F32))
        for c in store(i, slot):
            c.start()

        @pl.when(i == n_used - 1)
        def _():
            for c in store(i, slot):
                c.wait()

            @pl.when(i >= 1)
            def _():
                for c in store(i - 1, 1 - slot):
                    c.wait()


def _expert_ffn(block_e, n_used, xs, wg_all, wu_all, wd_all, layer):
    n_rows, n_slab, _ = xs.shape
    _, n_exp, d, ff = wg_all.shape
    nb = n_rows // EXPERT_BLOCK
    row_words = n_slab * LANES

    def wsel(i, be, nu):
        return (layer, be[jnp.minimum(i, nu[0] - 1)], 0, 0)

    return pl.pallas_call(
        _expert_kernel,
        grid_spec=pltpu.PrefetchScalarGridSpec(
            num_scalar_prefetch=2,
            grid=(nb,),
            in_specs=[
                pl.BlockSpec(memory_space=pl.ANY),
                pl.BlockSpec((None, None, d, ff), wsel),
                pl.BlockSpec((None, None, d, ff), wsel),
                pl.BlockSpec((None, None, ff, d), wsel),
            ],
            out_specs=pl.BlockSpec(memory_space=pl.ANY),
            scratch_shapes=[
                pltpu.VMEM((2, EXPERT_BLOCK, row_words), jnp.uint32),
                pltpu.VMEM((2, EXPERT_BLOCK, row_words), jnp.uint32),
                pltpu.VMEM((d, ff), BF16),
                pltpu.VMEM((d, ff), BF16),
                pltpu.VMEM((ff, d), BF16),
                pltpu.SemaphoreType.DMA((2,)),
                pltpu.SemaphoreType.DMA((2,)),
            ],
        ),
        out_shape=jax.ShapeDtypeStruct((n_rows, n_slab, LANES), jnp.uint32),
        compiler_params=_cparams("arbitrary"),
        name="expert_ffn",
    )(block_e, n_used, xs, wg_all, wu_all, wd_all)


COMBINE_TILE = 128


def _combine_kernel(d_first_ref, d_b_ref, d_next_ref, y_hbm, h_ref, tw_ref, x_ref, mod_ref, sg_ref,
                    su_ref, sd_ref, fg_ref, o_ref, ybuf_a, ybuf_b, sem_a, sem_b, *, final_norm):
    j = pl.program_id(0) * pl.num_programs(1) + pl.program_id(1)
    last = pl.num_programs(0) * pl.num_programs(1) - 1
    tile, row_words = ybuf_a.shape[1], ybuf_a.shape[2]

    def row_copy(d_ref, buf, sem, t, k):
        start = pl.multiple_of(d_ref[k, t], row_words)
        return pltpu.make_async_copy(y_hbm.at[pl.ds(start, row_words)], buf.at[k, t], sem)

    def start_rows(d_ref, buf, sem, lo, hi):
        for t in range(lo, hi):
            for k in range(TOP_K):
                row_copy(d_ref, buf, sem, t, k).start(priority=k % 2)

    def wait_all(d_ref, buf, sem):
        for t in range(tile):
            for k in range(TOP_K):
                row_copy(d_ref, buf, sem, 0, 0).wait()

    def shared_ffn(rows):
        hb = h_ref[rows, :].astype(BF16)
        g = jnp.dot(hb, sg_ref[...], preferred_element_type=F32)
        u = jnp.dot(hb, su_ref[...], preferred_element_type=F32)
        return jnp.dot(((g * _sigmoid(g)) * u).astype(BF16), sd_ref[...], preferred_element_type=F32)

    def finish(rows, acc, buf):
        acc_lo = acc[:, :row_words]
        acc_hi = acc[:, row_words:]
        for k in range(TOP_K):
            y_lo, y_hi = _unpack_halves(buf[k])
            wk = tw_ref[rows, k:k + 1]
            acc_lo = acc_lo + wk * y_lo
            acc_hi = acc_hi + wk * y_hi
        xo = x_ref[rows, :] + mod_ref[5:6, :] * jnp.concatenate([acc_lo, acc_hi], axis=1)
        if final_norm:
            ms = jnp.mean(xo * xo, axis=-1, keepdims=True)
            xo = xo * lax.rsqrt(ms + NORM_EPS) * fg_ref[...]
        o_ref[rows, :] = xo

    rows_a = slice(0, tile)
    rows_b = slice(tile, 2 * tile)

    @pl.when(j == 0)
    def _():
        def first(t, carry):
            for k in range(TOP_K):
                row_copy(d_first_ref, ybuf_a, sem_a, t, k).start()
            return carry
        lax.fori_loop(0, tile, first, 0)

    start_rows(d_b_ref, ybuf_b, sem_b, 0, tile // 2)
    acc_a = shared_ffn(rows_a)
    wait_all(d_b_ref, ybuf_a, sem_a)
    start_rows(d_b_ref, ybuf_b, sem_b, tile // 2, tile)
    finish(rows_a, acc_a, ybuf_a)
    acc_b = shared_ffn(rows_b)
    wait_all(d_b_ref, ybuf_b, sem_b)
    start_rows(d_next_ref, ybuf_a, sem_a, 0, tile)
    finish(rows_b, acc_b, ybuf_b)

    @pl.when(j == last)
    def _():
        wait_all(d_next_ref, ybuf_a, sem_a)


def _combine(dest, y, h, tw, x, mod, sg_all, su_all, sd_all, layer, fg, final_norm):
    bn, s, d = x.shape
    ff = sg_all.shape[2]
    tile = COMBINE_TILE
    tm = 2 * tile
    assert s % tm == 0
    per_b = s // tm
    n_tiles = bn * s // tile
    blk = pl.BlockSpec((None, tm, d), lambda bi, i: (bi, i, 0))

    def smem_tile(index):
        return pl.BlockSpec((TOP_K, tile), lambda bi, i: (0, index(bi * per_b + i)),
                            memory_space=pltpu.SMEM)

    return pl.pallas_call(
        functools.partial(_combine_kernel, final_norm=final_norm),
        grid=(bn, per_b),
        in_specs=[
            smem_tile(lambda j: 0),
            smem_tile(lambda j: 2 * j + 1),
            smem_tile(lambda j: jnp.minimum(2 * j + 2, n_tiles - 1)),
            pl.BlockSpec(memory_space=pl.ANY),
            blk,
            pl.BlockSpec((None, tm, TOP_K), lambda bi, i: (bi, i, 0)),
            blk,
            pl.BlockSpec((None, 6, d), lambda bi, i: (bi, 0, 0)),
            pl.BlockSpec((None, d, ff), lambda bi, i: (layer, 0, 0)),
            pl.BlockSpec((None, d, ff), lambda bi, i: (layer, 0, 0)),
            pl.BlockSpec((None, ff, d), lambda bi, i: (layer, 0, 0)),
            pl.BlockSpec((1, d), lambda bi, i: (0, 0)),
        ],
        out_specs=blk,
        out_shape=jax.ShapeDtypeStruct((bn, s, d), F32),
        scratch_shapes=[pltpu.VMEM((TOP_K, tile, d // 2), jnp.uint32),
                        pltpu.VMEM((TOP_K, tile, d // 2), jnp.uint32),
                        pltpu.SemaphoreType.DMA(()), pltpu.SemaphoreType.DMA(())],
        compiler_params=_cparams("arbitrary", "arbitrary"),
        name="combine",
    )(dest, dest, dest, y, h, tw, x, mod, sg_all, su_all, sd_all, fg.reshape(1, d))


def _moe_ffn(h, hp, x, mod, router_w, router_bias, exp_gate, exp_up, exp_down, sh_gate, sh_up, sh_down,
             layer, final_g, final_norm):
    bn, s, d = h.shape
    t = bn * s
    n_exp = router_w.shape[1]
    h2d = h.reshape(t, d)
    top_idx, top_w, rank, counts = _router(h2d, router_w, router_bias)
    cnt = counts.reshape(n_exp).astype(jnp.int32)
    padded = (cnt + EXPERT_BLOCK - 1) // EXPERT_BLOCK * EXPERT_BLOCK
    pends = jnp.cumsum(padded)
    pstarts = pends - padded
    experts = jnp.arange(n_exp, dtype=jnp.int32).reshape(n_exp, 1, 1)
    dest = jnp.sum(jnp.where(top_idx[None] == experts, pstarts.reshape(n_exp, 1, 1), 0), axis=0) + rank
    row_words = d // 2
    dest_words = dest * row_words
    nb = (t * TOP_K) // EXPERT_BLOCK + n_exp
    block_start = jnp.arange(nb, dtype=jnp.int32) * EXPERT_BLOCK
    block_e = jnp.minimum(jnp.sum(pends[None, :] <= block_start[:, None], axis=1), n_exp - 1)
    n_used = (pends[-1] // EXPERT_BLOCK).reshape(1).astype(jnp.int32)
    tail_start = jnp.maximum(pends - EXPERT_BLOCK, 0).astype(jnp.int32)
    n_rows = nb * EXPERT_BLOCK
    n_slab = row_words // LANES
    xs = _dispatch(tail_start, dest_words, hp.reshape(t, row_words), n_rows)
    y = _expert_ffn(block_e.astype(jnp.int32), n_used, xs.reshape(n_rows, n_slab, LANES),
                    exp_gate, exp_up, exp_down, layer)
    tw = jnp.transpose(top_w).reshape(bn, s, TOP_K)
    return _combine(dest_words, y.reshape(n_rows * row_words), h, tw, x, mod, sh_gate, sh_up,
                    sh_down, layer, final_g, final_norm)


def kernel(x, c, ada_w, ada_b, norm_mix_g, w_in, b_in, sgu_ln_g, sgu_ln_b, sgu_w, sgu_b, lru_conv_w, lru_conv_b, lru_wa, lru_ba, lru_wx, lru_bx, lru_lambda, conv_w, conv_b, conv_ln_g, conv_ln_b, w_out, b_out, norm_ffn_g, router_w, router_bias, exp_gate, exp_up, exp_down, sh_gate, sh_up, sh_down, final_norm_g):
    depth = ada_w.shape[0]
    bn, s, d = x.shape
    mod_all = _ada_modulation(c, ada_w, ada_b).reshape(depth, bn, 6, d)
    w_in_b, w_out_b = w_in.astype(BF16), w_out.astype(BF16)
    exp_b = [exp_gate, exp_up, exp_down]
    sh_b = [w.astype(BF16) for w in (sh_gate, sh_up, sh_down)]
    for l in range(depth):
        mod = mod_all[l]
        proj = _in_projection(x, mod, norm_mix_g[l], w_in_b, l, b_in[l])
        y = _hybrid_mixer(proj, sgu_ln_g[l], sgu_ln_b[l], sgu_w[l], sgu_b[l], lru_conv_w[l],
                          lru_conv_b[l], lru_wa[l], lru_ba[l], lru_wx[l], lru_bx[l], lru_lambda[l],
                          conv_w[l], conv_b[l], conv_ln_g[l], conv_ln_b[l])
        x, h, hp = _out_projection(y, w_out_b, l, b_out[l], x, mod, norm_ffn_g[l])
        x = _moe_ffn(h, hp, x, mod, router_w[l], router_bias[l], *exp_b, *sh_b, l, final_norm_g,
                     l == depth - 1)
    return x
```
